```python
import math
import jax, jax.numpy as jnp
from jax import lax
import numpy as np

D_MODEL = 2048
BATCH = 32
SEQ = 256
DEPTH = 2
DEC_BATCH = 2
DEC_SEQ = 4096
PAST_LEN = 512

GRID_W = 64
MIX_W = D_MODEL
A_HEADS = 8
QK_DIM = 64
V_DIM = 2 * QK_DIM
Q_W = A_HEADS * 2 * QK_DIM
A_WIDTH = A_HEADS * V_DIM
B_GROUPS = 4
CHUNK = 128
B_CH = (MIX_W // 4) // B_GROUPS
B_WIDTH = B_GROUPS * B_CH
POOL_WINDOWS = (2, 4, 8, 16)
N_POOL = len(POOL_WINDOWS)
C_CH = (MIX_W // 4) // N_POOL
C_WIDTH = N_POOL * C_CH
IN_W = 2 * Q_W + A_WIDTH + 2 * B_WIDTH + C_WIDTH
ROPE_BASE = 10000.0
N_FREQ = QK_DIM // 4
Q_BLOCK = 128
N_EXPERTS = 32
TOP_K = 4
D_FF = D_MODEL
SWIGLU_ALPHA = 1.702
SWIGLU_LIMIT = 7.0
MOE_BLOCK = 128
EPS = 1e-6
F32 = jnp.float32

kernel_name = "hybrid_diff_gmlp_pool_moe_dit_step"


def rmsnorm(x, g):
    xf = x.astype(F32)
    y = xf * lax.rsqrt(jnp.mean(xf * xf, axis=-1, keepdims=True) + EPS)
    return (y * g.astype(F32)).astype(x.dtype)


def adaln(cond, w, b):
    mod = jax.nn.silu(cond) @ w + b
    return [m[:, None, :] for m in jnp.split(mod, 6, axis=-1)]


def axial_rope(n_tok):
    n_rows = n_tok // GRID_W
    row = jnp.repeat(jnp.arange(n_rows), GRID_W).astype(F32)
    col = jnp.tile(jnp.arange(GRID_W), n_rows).astype(F32)
    inv = 1.0 / (ROPE_BASE ** (jnp.arange(N_FREQ, dtype=F32) / N_FREQ))
    ang = jnp.stack([row[:, None] * inv, col[:, None] * inv], axis=1)
    return jnp.cos(ang), jnp.sin(ang)


def apply_rope(x, cos, sin):
    shp = x.shape
    xr = x.reshape(shp[:-1] + (2, 2, N_FREQ))
    x1, x2 = xr[..., 0, :], xr[..., 1, :]
    c = cos[None, :, None, None, :, :].astype(x.dtype)
    s = sin[None, :, None, None, :, :].astype(x.dtype)
    out = jnp.stack([x1 * c - x2 * s, x2 * c + x1 * s], axis=-2)
    return out.reshape(shp)


def diff_attention(q, k, v, lam, lam_init, ln_g):
    B, Sq = q.shape[0], q.shape[1]
    nb = Sq // Q_BLOCK
    qb = jnp.moveaxis(q.reshape(B, nb, Q_BLOCK, A_HEADS, 2, QK_DIM), 1, 0)
    scale = QK_DIM ** -0.5

    def one_block(qblk):
        s = jnp.einsum('bqhnd,bkhnd->bnhqk', qblk, k, preferred_element_type=F32) * scale
        p = jax.nn.softmax(s, axis=-1)
        a = p[:, 0] - lam * p[:, 1]
        return jnp.einsum('bhqk,bkhv->bqhv', a.astype(v.dtype), v)

    o = lax.map(one_block, qb)
    o = jnp.moveaxis(o, 0, 1).reshape(B, Sq, A_HEADS, V_DIM)
    o = rmsnorm(o, ln_g) * (1.0 - lam_init)
    return o.reshape(B, Sq, A_WIDTH)


def chunk_gmlp(zb, norm_g, ws, bs):
    B, S, _ = zb.shape
    zb = jax.nn.gelu(zb)
    u, v = zb[..., :B_WIDTH], zb[..., B_WIDTH:]
    v = rmsnorm(v.reshape(B, S, B_GROUPS, B_CH), norm_g.reshape(B_GROUPS, B_CH))
    v = v.reshape(B, S // CHUNK, CHUNK, B_GROUPS, B_CH)
    mixed = jnp.einsum('gpq,bnqgc->bnpgc', ws, v) + bs.T[None, None, :, :, None]
    return u * mixed.reshape(B, S, B_WIDTH)


def multiscale_pool(zc, pool_w, pool_scale):
    B, S, _ = zc.shape
    xf = zc.astype(F32)
    cs = jnp.concatenate([jnp.zeros((B, 1, C_WIDTH), F32), jnp.cumsum(xf, axis=1)], axis=1)
    t = jnp.arange(S)
    means = []
    for gi, w in enumerate(POOL_WINDOWS):
        lo = jnp.clip(t - w // 2, 0, S - 1)
        hi = jnp.clip(t + w // 2 - 1, 0, S - 1)
        csg = cs[..., gi * C_CH:(gi + 1) * C_CH]
        cnt = (hi - lo + 1).astype(F32)
        means.append((csg[:, hi + 1] - csg[:, lo]) / cnt[None, :, None])
    pooled = (jnp.concatenate(means, axis=-1) - xf).astype(zc.dtype)
    y = jnp.einsum('bsgc,gcd->bsgd', pooled.reshape(B, S, N_POOL, C_CH), pool_w)
    return y.reshape(B, S, C_WIDTH) * pool_scale


def mixer(h, l, P, ctx_kv, rope):
    B, S, _ = h.shape
    z = h @ P['w_in'][l]
    o1, o2 = Q_W, 2 * Q_W
    o3 = o2 + A_WIDTH
    o4 = o3 + 2 * B_WIDTH
    q = z[..., :o1].reshape(B, S, A_HEADS, 2, QK_DIM)
    k = z[..., o1:o2].reshape(B, S, A_HEADS, 2, QK_DIM)
    v = z[..., o2:o3].reshape(B, S, A_HEADS, V_DIM)
    zb, zc = z[..., o3:o4], z[..., o4:]
    lam_init = 0.8 - 0.6 * math.exp(-0.3 * l)
    lam = (jnp.exp(jnp.sum(P['lam_q1'][l].astype(F32) * P['lam_k1'][l].astype(F32)))
           - jnp.exp(jnp.sum(P['lam_q2'][l].astype(F32) * P['lam_k2'][l].astype(F32))) + lam_init)
    if ctx_kv is None:
        k_all, v_all = k, v
    else:
        cos, sin = rope
        q = apply_rope(q, cos, sin)
        k = apply_rope(k, cos, sin)
        ck, cv = ctx_kv
        ck = ck.reshape(ck.shape[0], ck.shape[1], A_HEADS, 2, QK_DIM).astype(k.dtype)
        k_all = jnp.concatenate([k, ck], axis=1)
        v_all = jnp.concatenate([v, cv.astype(v.dtype)], axis=1)
    oa = diff_attention(q, k_all, v_all, lam, lam_init, P['diff_ln_g'][l])
    ob = chunk_gmlp(zb, P['gmlp_norm_g'][l], P['gmlp_ws'][l], P['gmlp_bs'][l])
    oc = multiscale_pool(zc, P['pool_w'][l], P['pool_scale'][l])
    out = jnp.concatenate([oa, ob, oc], axis=-1) @ P['w_out'][l]
    return out, k.reshape(B, S, A_HEADS, 2 * QK_DIM), v


def moe_ffn(h, rw, rb, w_gu, b_gu, w_dn, b_dn):
    N, D = h.shape
    logits = (h @ rw).astype(F32) + rb.astype(F32)
    top_v, top_i = lax.top_k(logits, TOP_K)
    gates = jax.nn.softmax(top_v, axis=-1)
    NK = N * TOP_K
    flat_e = top_i.reshape(NK)
    order = jnp.argsort(flat_e)
    sorted_e = flat_e[order]
    counts = jnp.bincount(flat_e, length=N_EXPERTS)
    padded = (counts + MOE_BLOCK - 1) // MOE_BLOCK * MOE_BLOCK
    start = jnp.cumsum(counts) - counts
    ends_p = jnp.cumsum(padded)
    pstart = ends_p - padded
    dest = pstart[sorted_e] + jnp.arange(NK) - start[sorted_e]
    n_blocks = -(-NK // MOE_BLOCK) + N_EXPERTS
    tok = order // TOP_K
    buf_tok = jnp.zeros((n_blocks * MOE_BLOCK,), jnp.int32).at[dest].set(tok)
    blk_e = jnp.minimum(jnp.searchsorted(ends_p, jnp.arange(n_blocks) * MOE_BLOCK, side='right'), N_EXPERTS - 1)
    xb = h[buf_tok].reshape(n_blocks, MOE_BLOCK, D)

    def expert_block(args):
        xblk, e = args
        gu = xblk @ w_gu[e] + b_gu[e]
        g = jnp.minimum(gu[:, :D_FF], SWIGLU_LIMIT)
        u = jnp.clip(gu[:, D_FF:], -SWIGLU_LIMIT, SWIGLU_LIMIT)
        a = g * jax.nn.sigmoid(SWIGLU_ALPHA * g) * (u + 1.0)
        return a @ w_dn[e] + b_dn[e]

    yb = lax.map(expert_block, (xb, blk_e)).reshape(-1, D)
    w_slot = gates.reshape(NK)[order].astype(h.dtype)
    return jnp.zeros_like(h).at[tok].add(yb[dest] * w_slot[:, None])


def trunk_layer(x, cond, l, P, ctx_kv, rope):
    sh1, sc1, g1, sh2, sc2, g2 = adaln(cond, P['w_ada'][l], P['b_ada'][l])
    h = rmsnorm(x, P['norm1_g'][l]) * (1.0 + sc1) + sh1
    mix, k, v = mixer(h, l, P, ctx_kv, rope)
    x = x + g1 * mix
    h = rmsnorm(x, P['norm2_g'][l]) * (1.0 + sc2) + sh2
    B, S, D = h.shape
    ff = moe_ffn(h.reshape(B * S, D), P['router_w'][l], P['router_b'][l], P['exp_w_gu'][l],
                 P['exp_b_gu'][l], P['exp_w_down'][l], P['exp_b_down'][l])
    x = x + g2 * ff.reshape(B, S, D)
    return x, k, v


def setup_inputs(seed: int = 0) -> dict:
    key = jax.random.key(seed)
    ks = iter(jax.random.split(key, 40))
    nrm = lambda shape, s=1.0: jax.random.normal(next(ks), shape, F32) * s
    gain = lambda shape: 1.0 + nrm(shape, 0.02)
    D = D_MODEL
    return {
        'x_prompt': nrm((BATCH, SEQ, D)),
        'x_sample': nrm((DEC_BATCH, DEC_SEQ, D)),
        'cache_k': nrm((DEC_BATCH, DEPTH, PAST_LEN, A_HEADS, 2 * QK_DIM)),
        'cache_v': nrm((DEC_BATCH, DEPTH, PAST_LEN, A_HEADS, V_DIM)),
        'c': nrm((DEC_BATCH, D)),
        'c_ctx': nrm((D,)),
        'norm1_g': gain((DEPTH, D)),
        'norm2_g': gain((DEPTH, D)),
        'w_ada': nrm((DEPTH, D, 6 * D), 0.5 * D ** -0.5),
        'b_ada': nrm((DEPTH, 6 * D), 0.02),
        'w_in': nrm((DEPTH, D, IN_W), D ** -0.5),
        'w_out': nrm((DEPTH, MIX_W, D), MIX_W ** -0.5),
        'lam_q1': nrm((DEPTH, QK_DIM), 0.1),
        'lam_k1': nrm((DEPTH, QK_DIM), 0.1),
        'lam_q2': nrm((DEPTH, QK_DIM), 0.1),
        'lam_k2': nrm((DEPTH, QK_DIM), 0.1),
        'diff_ln_g': gain((DEPTH, V_DIM)),
        'gmlp_norm_g': gain((DEPTH, B_WIDTH)),
        'gmlp_ws': nrm((DEPTH, B_GROUPS, CHUNK, CHUNK), CHUNK ** -0.5),
        'gmlp_bs': gain((DEPTH, B_GROUPS, CHUNK)),
        'pool_w': nrm((DEPTH, N_POOL, C_CH, C_CH), C_CH ** -0.5),
        'pool_scale': gain((DEPTH, C_WIDTH)),
        'router_w': nrm((DEPTH, D, N_EXPERTS), D ** -0.5),
        'router_b': nrm((DEPTH, N_EXPERTS), 0.01),
        'exp_w_gu': nrm((DEPTH, N_EXPERTS, D, 2 * D_FF), D ** -0.5),
        'exp_b_gu': nrm((DEPTH, N_EXPERTS, 2 * D_FF), 0.01),
        'exp_w_down': nrm((DEPTH, N_EXPERTS, D_FF, D), D_FF ** -0.5),
        'exp_b_down': nrm((DEPTH, N_EXPERTS, D), 0.01),
        'final_g': gain((D,)),
    }


def reference(x_prompt, x_sample, cache_k, cache_v, c, c_ctx, norm1_g, norm2_g, w_ada, b_ada,
              w_in, w_out, lam_q1, lam_k1, lam_q2, lam_k2, diff_ln_g, gmlp_norm_g, gmlp_ws,
              gmlp_bs, pool_w, pool_scale, router_w, router_b, exp_w_gu, exp_b_gu,
              exp_w_down, exp_b_down, final_g):
    P = dict(norm1_g=norm1_g, norm2_g=norm2_g, w_ada=w_ada, b_ada=b_ada, w_in=w_in, w_out=w_out,
             lam_q1=lam_q1, lam_k1=lam_k1, lam_q2=lam_q2, lam_k2=lam_k2, diff_ln_g=diff_ln_g,
             gmlp_norm_g=gmlp_norm_g, gmlp_ws=gmlp_ws, gmlp_bs=gmlp_bs, pool_w=pool_w,
             pool_scale=pool_scale, router_w=router_w, router_b=router_b, exp_w_gu=exp_w_gu,
             exp_b_gu=exp_b_gu, exp_w_down=exp_w_down, exp_b_down=exp_b_down)
    xp = x_prompt
    cond_ctx = c_ctx[None, :]
    ks, vs = [], []
    for l in range(DEPTH):
        xp, k_l, v_l = trunk_layer(xp, cond_ctx, l, P, None, None)
        ks.append(k_l)
        vs.append(v_l)
    y_prompt = rmsnorm(xp, final_g)
    new_cache_k = jnp.stack(ks, axis=1)
    new_cache_v = jnp.stack(vs, axis=1)
    rope = axial_rope(x_sample.shape[1])
    xs = x_sample
    for l in range(DEPTH):
        xs, _, _ = trunk_layer(xs, c, l, P, (cache_k[:, l], cache_v[:, l]), rope)
    y_sample = rmsnorm(xs, final_g)
    return (y_prompt, y_sample, new_cache_k, new_cache_v)
```

```python
import functools
import math

import numpy as np
import jax
import jax.numpy as jnp
from jax import lax
from jax.experimental import pallas as pl
from jax.experimental.pallas import tpu as pltpu

F32 = jnp.float32
BF16 = jnp.bfloat16
I32 = jnp.int32
U32 = jnp.uint32

GRID_W = 64
A_HEADS = 8
QK_DIM = 64
V_DIM = 2 * QK_DIM
HEAD_W = 2 * QK_DIM
Q_W = A_HEADS * HEAD_W
A_WIDTH = A_HEADS * V_DIM
B_GROUPS = 4
CHUNK = 128
B_CH = 128
B_WIDTH = B_GROUPS * B_CH
POOL_WINDOWS = (2, 4, 8, 16)
N_POOL = len(POOL_WINDOWS)
C_CH = 128
C_WIDTH = N_POOL * C_CH
ROPE_BASE = 10000.0
N_FREQ = QK_DIM // 4
TOP_K = 4
SWIGLU_ALPHA = 1.702
SWIGLU_LIMIT = 7.0
EPS = 1e-6
QK_SCALE = QK_DIM ** -0.5

LANES = 128
TILE = 256
HALO = 16
TM_PROJ = 512
TM_MOE = 1024
SUB_MOE = 256
TF_MOE = 256
TQ_ATT = 512
VMEM_LIMIT = 56 * 1024 * 1024


def _cparams(sem, vmem=VMEM_LIMIT):
    return pltpu.CompilerParams(dimension_semantics=sem, vmem_limit_bytes=vmem)


def _norm_mod(x, g, shift, scale):
    ms = jnp.mean(x * x, axis=-1, keepdims=True)
    return x * lax.rsqrt(ms + EPS) * g * (1.0 + scale) + shift


def _rms_lanes(x):
    return x * lax.rsqrt(jnp.mean(x * x, axis=-1, keepdims=True) + EPS)


def _dot(a, b):
    return jnp.dot(a, b, preferred_element_type=F32)


def _dot_nt(a, b):
    return lax.dot_general(a, b, (((1,), (1,)), ((), ())), preferred_element_type=F32)


def _ada_body(ct_ref, w_ref, b_ref, o_ref, *, kc):
    d, ng = ct_ref.shape
    tn = w_ref.shape[-1]

    def body(c, accs):
        k0 = pl.multiple_of(c * kc, kc)
        w = w_ref[0, pl.ds(k0, kc), :]
        s = jax.nn.silu(ct_ref[pl.ds(k0, kc), :])
        return tuple(
            acc + jnp.sum((w * s[:, gi:gi + 1]).reshape(kc // 8, 8, tn), axis=0)
            for gi, acc in enumerate(accs))

    accs = lax.fori_loop(0, d // kc, body, tuple(jnp.zeros((8, tn), F32) for _ in range(ng)))
    for gi in range(ng):
        o_ref[0, gi:gi + 1, :] = jnp.sum(accs[gi], axis=0, keepdims=True) + b_ref[0]


def _ada(cond_t, w_ada, b_ada):
    d, ng = cond_t.shape
    nl, _, n = w_ada.shape
    tn = 1024
    return pl.pallas_call(
        functools.partial(_ada_body, kc=32),
        grid=(nl, n // tn),
        in_specs=[
            pl.BlockSpec((d, ng), lambda l, j: (0, 0)),
            pl.BlockSpec((1, d, tn), lambda l, j: (l, 0, j)),
            pl.BlockSpec((1, 1, tn), lambda l, j: (l, 0, j)),
        ],
        out_specs=pl.BlockSpec((1, ng, tn), lambda l, j: (l, 0, j)),
        out_shape=jax.ShapeDtypeStruct((nl, ng, n), F32),
        compiler_params=_cparams(("parallel", "parallel")),
        name="ada",
    )(cond_t, w_ada, b_ada.reshape(nl, 1, n))


def _inproj_body(x_ref, mod_ref, g_ref, w_ref, z_ref, h_scr, *, rc):
    @pl.when(pl.program_id(1) == 0)
    def _():
        shift = mod_ref[0, 0:1, :]
        scale = mod_ref[0, 1:2, :]
        g = g_ref[...]

        def body(c, carry):
            r0 = pl.multiple_of(c * rc, rc)
            h = _norm_mod(x_ref[pl.ds(r0, rc), :], g, shift, scale)
            h_scr[pl.ds(r0, rc), :] = h.astype(BF16)
            return carry

        lax.fori_loop(0, x_ref.shape[0] // rc, body, 0)

    z_ref[...] = _dot(h_scr[...], w_ref[...])


def _group_of_block(i, rows, t_ctx, ds):
    r = i * rows
    return jnp.where(r < t_ctx, 0, 1 + (r - t_ctx) // ds)


def _inproj(x, mod, g1, w_in, *, t_ctx, ds):
    t, d = x.shape
    n = w_in.shape[1]
    tm, tn = TM_PROJ, 1536
    assert t_ctx % tm == 0 and ds % tm == 0 and n % tn == 0
    grp = functools.partial(_group_of_block, rows=tm, t_ctx=t_ctx, ds=ds)
    return pl.pallas_call(
        functools.partial(_inproj_body, rc=64),
        grid=(t // tm, n // tn),
        in_specs=[
            pl.BlockSpec((tm, d), lambda i, j: (i, 0)),
            pl.BlockSpec((1, 6, d), lambda i, j: (grp(i), 0, 0)),
            pl.BlockSpec((1, d), lambda i, j: (0, 0)),
            pl.BlockSpec((d, tn), lambda i, j: (0, j)),
        ],
        out_specs=pl.BlockSpec((tm, tn), lambda i, j: (i, j)),
        out_shape=jax.ShapeDtypeStruct((t, n), F32),
        scratch_shapes=[pltpu.VMEM((tm, d), BF16)],
        compiler_params=_cparams(("parallel", "arbitrary")),
        name="inproj",
    )(x, mod, g1, w_in)


def _lam_value(lamp_ref, lam_init):
    lp = lamp_ref[...]
    a = jnp.sum(lp[0:1] * lp[1:2], keepdims=True)
    b = jnp.sum(lp[2:3] * lp[3:4], keepdims=True)
    return jnp.exp(a) - jnp.exp(b) + lam_init


def _split_halves(q):
    first = lax.broadcasted_iota(I32, (1, HEAD_W), 1) < QK_DIM
    return jnp.where(first, q, 0.0).astype(BF16), jnp.where(first, 0.0, q).astype(BF16)


def _attn_ctx_body(lamp_ref, lng_ref, q_ref, k_ref, v_ref, o_ref, *, lam_init):
    lam = _lam_value(lamp_ref, lam_init)
    g = lng_ref[...] * (1.0 - lam_init)
    for h in range(A_HEADS):
        hs = slice(h * HEAD_W, (h + 1) * HEAD_W)
        k = k_ref[:, hs].astype(BF16)
        v = v_ref[:, hs].astype(BF16)
        ps = []
        for qh in _split_halves(q_ref[:, hs] * QK_SCALE):
            s = _dot_nt(qh, k)
            e = jnp.exp(s - jnp.max(s, axis=-1, keepdims=True))
            ps.append(e * (1.0 / jnp.sum(e, axis=-1, keepdims=True)))
        o = _dot((ps[0] - lam * ps[1]).astype(BF16), v)
        o_ref[:, hs] = (_rms_lanes(o) * g).astype(o_ref.dtype)


def _attn_ctx(z, lamp, lng, *, nb, s, lam_init):
    return pl.pallas_call(
        functools.partial(_attn_ctx_body, lam_init=lam_init),
        grid=(nb,),
        in_specs=[
            pl.BlockSpec(lamp.shape, lambda b: (0, 0)),
            pl.BlockSpec((1, V_DIM), lambda b: (0, 0)),
            pl.BlockSpec((s, Q_W), lambda b: (b, 0)),
            pl.BlockSpec((s, Q_W), lambda b: (b, 1)),
            pl.BlockSpec((s, A_WIDTH), lambda b: (b, 2)),
        ],
        out_specs=pl.BlockSpec((s, A_WIDTH), lambda b: (b, 0)),
        out_shape=jax.ShapeDtypeStruct((nb * s, A_WIDTH), BF16),
        compiler_params=_cparams(("parallel",)),
        name="attn_ctx",
    )(lamp, lng, z, z, z)


def _rope(x, c, sa, sb):
    return x * c + pltpu.roll(x, HEAD_W - N_FREQ, 1) * sa + pltpu.roll(x, N_FREQ, 1) * sb


def _attn_lat_body(lamp_ref, lng_ref, q_ref, k_ref, v_ref, ck_ref, cv_ref,
                   cq_ref, saq_ref, sbq_ref, ckk_ref, sak_ref, sbk_ref,
                   o_ref, kall, vall, s_scr, *, lam_init, kc, rc):
    ds = k_ref.shape[0]
    p = ck_ref.shape[2]
    tq = q_ref.shape[0]
    nch = (ds + p) // kc

    @pl.when(pl.program_id(2) == 0)
    def _():
        def body(c, carry):
            r0 = pl.multiple_of(c * rc, rc)
            rs = pl.ds(r0, rc)
            kall[rs, :] = _rope(k_ref[rs, :], ckk_ref[rs, :], sak_ref[rs, :], sbk_ref[rs, :]).astype(BF16)
            vall[rs, :] = v_ref[rs, :].astype(BF16)
            return carry

        lax.fori_loop(0, ds // rc, body, 0)
        kall[ds:ds + p, :] = ck_ref[0, 0].astype(BF16)
        vall[ds:ds + p, :] = cv_ref[0, 0].astype(BF16)

    lam = _lam_value(lamp_ref, lam_init)
    q = _rope(q_ref[...], cq_ref[...], saq_ref[...], sbq_ref[...]) * QK_SCALE
    outs = []
    for n, qh in enumerate(_split_halves(q)):
        m = jnp.full((tq, 1), -jnp.inf, F32)
        for c in range(nch):
            s = _dot_nt(qh, kall[c * kc:(c + 1) * kc, :])
            s_scr[n, :, c * kc:(c + 1) * kc] = s
            m = jnp.maximum(m, jnp.max(s, axis=-1, keepdims=True))
        l = jnp.zeros((tq, 1), F32)
        o = jnp.zeros((tq, V_DIM), F32)
        for c in range(nch):
            e = jnp.exp(s_scr[n, :, c * kc:(c + 1) * kc] - m)
            l = l + jnp.sum(e, axis=-1, keepdims=True)
            o = o + _dot(e.astype(BF16), vall[c * kc:(c + 1) * kc, :])
        outs.append(o * (1.0 / l))
    o = outs[0] - lam * outs[1]
    o_ref[...] = (_rms_lanes(o) * (lng_ref[...] * (1.0 - lam_init))).astype(o_ref.dtype)


def _attn_lat(z, cache_k, cache_v, rope_tabs, lamp, lng, *, layer, t_ctx, nb, ds, lam_init):
    p = cache_k.shape[2]
    tq = min(TQ_ATT, ds)
    sk = ds + p
    kc = 512 if sk % 512 == 0 else LANES
    assert t_ctx % ds == 0 and ds % tq == 0 and sk % kc == 0
    seq0, q0 = t_ctx // ds, t_ctx // tq
    nq = ds // tq
    qh, kh, vh = 0, Q_W // HEAD_W, 2 * Q_W // HEAD_W
    tab_q = pl.BlockSpec((tq, HEAD_W), lambda b, h, qi: (qi, 0))
    tab_k = pl.BlockSpec((ds, HEAD_W), lambda b, h, qi: (0, 0))
    cos, sa, sb = rope_tabs
    return pl.pallas_call(
        functools.partial(_attn_lat_body, lam_init=lam_init, kc=kc, rc=256),
        grid=(nb, A_HEADS, nq),
        in_specs=[
            pl.BlockSpec(lamp.shape, lambda b, h, qi: (0, 0)),
            pl.BlockSpec((1, V_DIM), lambda b, h, qi: (0, 0)),
            pl.BlockSpec((tq, HEAD_W), lambda b, h, qi: (q0 + b * nq + qi, qh + h)),
            pl.BlockSpec((ds, HEAD_W), lambda b, h, qi: (seq0 + b, kh + h)),
            pl.BlockSpec((ds, HEAD_W), lambda b, h, qi: (seq0 + b, vh + h)),
            pl.BlockSpec((1, 1, p, HEAD_W), lambda b, h, qi: (b, layer, 0, h)),
            pl.BlockSpec((1, 1, p, V_DIM), lambda b, h, qi: (b, layer, 0, h)),
            tab_q, tab_q, tab_q, tab_k, tab_k, tab_k,
        ],
        out_specs=pl.BlockSpec((tq, V_DIM), lambda b, h, qi: (b * nq + qi, h)),
        out_shape=jax.ShapeDtypeStruct((nb * ds, A_WIDTH), BF16),
        scratch_shapes=[
            pltpu.VMEM((sk, HEAD_W), BF16),
            pltpu.VMEM((sk, V_DIM), BF16),
            pltpu.VMEM((2, tq, sk), F32),
        ],
        compiler_params=_cparams(("parallel", "parallel", "arbitrary")),
        name="attn_lat",
    )(lamp, lng, z, z, z, cache_k, cache_v, cos, sa, sb, cos, sa, sb)


def _rope_tables(n_tok):
    n_rows = n_tok // GRID_W
    row = jnp.repeat(jnp.arange(n_rows), GRID_W).astype(F32)
    col = jnp.tile(jnp.arange(GRID_W), n_rows).astype(F32)
    inv = 1.0 / (ROPE_BASE ** (jnp.arange(N_FREQ, dtype=F32) / N_FREQ))
    ang = jnp.stack([row[:, None] * inv, col[:, None] * inv], axis=1)
    cos, sin = jnp.cos(ang), jnp.sin(ang)
    zero = jnp.zeros_like(sin)

    def lanes(first, second):
        per_axis = jnp.concatenate([first, second], axis=-1)
        return jnp.tile(per_axis.reshape(n_tok, 2 * 2 * N_FREQ), (1, 2))

    return lanes(cos, cos), lanes(-sin, zero), lanes(zero, sin)


def _pack_pair(hi, lo):
    hb = lax.bitcast_convert_type(hi.astype(BF16).astype(F32), U32)
    lb = lax.bitcast_convert_type(lo.astype(BF16).astype(F32), U32)
    return hb | (lb >> 16)


def _unpack_pair(u):
    hi = lax.bitcast_convert_type(u & jnp.uint32(0xFFFF0000), F32)
    lo = lax.bitcast_convert_type(u << 16, F32)
    return hi.astype(BF16), lo.astype(BF16)


def _hi_lo(x):
    hi = x.astype(BF16)
    return hi, (x - hi.astype(F32)).astype(BF16)


def _mixpost_body(oa_ref, zb_ref, zc_ref, zp_ref, zn_ref, x_ref, mod_ref, g2_ref, wout_ref,
                  gng_ref, ws_ref, gb_ref, pw_ref, ps_ref, band_ref, bandp_ref, bandn_ref,
                  rw_ref, rb_ref, xo_ref, h_ref, lg_ref, cat_scr, *, n_ctx_tiles, tps_ctx, tps_lat):
    i = pl.program_id(0)
    is_ctx = i < n_ctx_tiles
    pos = jnp.where(is_ctx, i % tps_ctx, (i - n_ctx_tiles) % tps_lat)
    first = pos == 0
    last = pos == jnp.where(is_ctx, tps_ctx, tps_lat) - 1

    cat_scr[:, 0:A_WIDTH] = oa_ref[...]

    zb = jax.nn.gelu(zb_ref[...])
    u, v = zb[:, :B_WIDTH], zb[:, B_WIDTH:]
    for g in range(B_GROUPS):
        gs = slice(g * B_CH, (g + 1) * B_CH)
        vn = (_rms_lanes(v[:, gs]) * gng_ref[:, gs]).astype(BF16)
        for n in range(TILE // CHUNK):
            rs = slice(n * CHUNK, (n + 1) * CHUNK)
            mixed = _dot(ws_ref[g], vn[rs]) + gb_ref[:, gs]
            cat_scr[rs, A_WIDTH + g * B_CH:A_WIDTH + (g + 1) * B_CH] = (u[rs, gs] * mixed).astype(BF16)

    xc = zc_ref[...]
    xc_parts = _hi_lo(xc)
    xp_parts = _hi_lo(zp_ref[...] * jnp.where(first, 0.0, 1.0))
    xn_parts = _hi_lo(zn_ref[...] * jnp.where(last, 0.0, 1.0))
    r = lax.broadcasted_iota(I32, (TILE, 1), 0)
    for g, w in enumerate(POOL_WINDOWS):
        gs = slice(g * C_CH, (g + 1) * C_CH)
        half = w // 2
        acc = jnp.zeros((TILE, C_CH), F32)
        for part in xc_parts:
            acc = acc + _dot(band_ref[g], part[:, gs])
        for part in xp_parts:
            acc = acc + _dot(bandp_ref[g], part[:, gs])
        for part in xn_parts:
            acc = acc + _dot(bandn_ref[g], part[:, gs])
        left = jnp.where(first, jnp.minimum(half, r), half)
        right = jnp.where(last, jnp.minimum(half - 1, TILE - 1 - r), half - 1)
        cnt = (left + right + 1).astype(F32)
        pooled = acc / cnt - xc[:, gs]
        y = _dot(pooled.astype(BF16), pw_ref[g]) * ps_ref[:, gs]
        c0 = A_WIDTH + B_WIDTH + g * C_CH
        cat_scr[:, c0:c0 + C_CH] = y.astype(BF16)

    mix = _dot(cat_scr[...], wout_ref[...])
    x_new = x_ref[...] + mod_ref[0, 2:3, :] * mix
    xo_ref[...] = x_new
    h2 = _norm_mod(x_new, g2_ref[...], mod_ref[0, 3:4, :], mod_ref[0, 4:5, :])
    half_d = h2.shape[1] // 2
    h_ref[...] = _pack_pair(h2[:, :half_d], h2[:, half_d:])
    lg_ref[...] = jnp.dot(h2, rw_ref[...], precision=lax.Precision.HIGHEST,
                          preferred_element_type=F32) + rb_ref[...]


def _band_matrices():
    r = np.arange(TILE)[:, None]
    cur, prev, nxt = [], [], []
    for w in POOL_WINDOWS:
        half = w // 2
        c = np.arange(TILE)[None, :]
        cur.append((c >= r - half) & (c <= r + half - 1))
        ch = np.arange(HALO)[None, :]
        prev.append(ch - HALO >= r - half)
        nxt.append(TILE + ch <= r + half - 1)
    to = lambda m: jnp.asarray(np.stack(m).astype(np.float32), dtype=BF16)
    return to(cur), to(prev), to(nxt)


def _mixpost(oa, z, x, mod, g2, w_out, gng, ws, gb, pw, ps, rw, rb, *, t_ctx, s_ctx, ds):
    t, d = x.shape
    nt = t // TILE
    n_ctx_tiles = t_ctx // TILE
    tps_ctx, tps_lat = s_ctx // TILE, ds // TILE
    assert s_ctx % TILE == 0 and ds % TILE == 0
    grp = functools.partial(_group_of_block, rows=TILE, t_ctx=t_ctx, ds=ds)
    band, bandp, bandn = _band_matrices()
    zb_blk = (2 * Q_W + A_WIDTH) // (2 * B_WIDTH)
    zc_blk = (2 * Q_W + A_WIDTH + 2 * B_WIDTH) // C_WIDTH
    hpt = TILE // HALO
    const2 = lambda i: (0, 0)
    const3 = lambda i: (0, 0, 0)
    return pl.pallas_call(
        functools.partial(_mixpost_body, n_ctx_tiles=n_ctx_tiles, tps_ctx=tps_ctx, tps_lat=tps_lat),
        grid=(nt,),
        in_specs=[
            pl.BlockSpec((TILE, A_WIDTH), lambda i: (i, 0)),
            pl.BlockSpec((TILE, 2 * B_WIDTH), lambda i: (i, zb_blk)),
            pl.BlockSpec((TILE, C_WIDTH), lambda i: (i, zc_blk)),
            pl.BlockSpec((HALO, C_WIDTH), lambda i: (jnp.maximum(i * hpt - 1, 0), zc_blk)),
            pl.BlockSpec((HALO, C_WIDTH), lambda i: (jnp.minimum((i + 1) * hpt, nt * hpt - 1), zc_blk)),
            pl.BlockSpec((TILE, d), lambda i: (i, 0)),
            pl.BlockSpec((1, 6, d), lambda i: (grp(i), 0, 0)),
            pl.BlockSpec((1, d), const2),
            pl.BlockSpec(w_out.shape, const2),
            pl.BlockSpec((1, B_WIDTH), const2),
            pl.BlockSpec(ws.shape, const3),
            pl.BlockSpec(gb.shape, const2),
            pl.BlockSpec(pw.shape, const3),
            pl.BlockSpec((1, C_WIDTH), const2),
            pl.BlockSpec(band.shape, const3),
            pl.BlockSpec(bandp.shape, const3),
            pl.BlockSpec(bandn.shape, const3),
            pl.BlockSpec(rw.shape, const2),
            pl.BlockSpec(rb.shape, const2),
        ],
        out_specs=[
            pl.BlockSpec((TILE, d), lambda i: (i, 0)),
            pl.BlockSpec((TILE, d // 2), lambda i: (i, 0)),
            pl.BlockSpec((TILE, LANES), lambda i: (i, 0)),
        ],
        out_shape=[
            jax.ShapeDtypeStruct((t, d), F32),
            jax.ShapeDtypeStruct((t, d // 2), U32),
            jax.ShapeDtypeStruct((t, LANES), F32),
        ],
        scratch_shapes=[pltpu.VMEM((TILE, d), BF16)],
        compiler_params=_cparams(("parallel",)),
        name="mixpost",
    )(oa, z, z, z, z, x, mod, g2, w_out, gng, ws, gb, pw, ps, band, bandp, bandn, rw, rb)


def _route_body(lg_ref, tri_ref, idx_ref, gate_ref, rank_ref, cnt_ref, carry, *, n_exp):
    @pl.when(pl.program_id(0) == 0)
    def _():
        carry[...] = jnp.zeros_like(carry)

    lane = lax.broadcasted_iota(I32, lg_ref.shape, 1).astype(F32)
    l = jnp.where(lane < n_exp, lg_ref[...], -jnp.inf)
    vals, idxs, sels = [], [], []
    for _ in range(TOP_K):
        m = jnp.max(l, axis=-1, keepdims=True)
        idx = jnp.min(jnp.where(l == m, lane, float(LANES)), axis=-1, keepdims=True)
        sel = lane == idx
        vals.append(m)
        idxs.append(idx)
        sels.append(sel)
        l = jnp.where(sel, -jnp.inf, l)
    chosen = functools.reduce(jnp.logical_or, sels)
    chosen_f = jnp.where(chosen, 1.0, 0.0)
    prefix = carry[...] + _dot(tri_ref[...], chosen_f.astype(BF16))
    es = [jnp.exp(v - vals[0]) for v in vals]
    tot = functools.reduce(jnp.add, es)
    idx_o = jnp.zeros(lg_ref.shape, F32)
    gate_o = jnp.zeros(lg_ref.shape, F32)
    rank_o = jnp.zeros(lg_ref.shape, F32)
    for k in range(TOP_K):
        rank_k = jnp.sum(jnp.where(sels[k], prefix, 0.0), axis=-1, keepdims=True)
        idx_o = jnp.where(lane == k, idxs[k], idx_o)
        gate_o = jnp.where(lane == k, es[k] / tot, gate_o)
        rank_o = jnp.where(lane == k, rank_k, rank_o)
    idx_ref[...] = idx_o.astype(I32)
    gate_ref[...] = gate_o
    rank_ref[...] = rank_o.astype(I32)
    carry[...] = carry[...] + jnp.sum(chosen_f, axis=0, keepdims=True)
    cnt_ref[...] = carry[...]


def _route(logits, n_exp):
    t = logits.shape[0]
    tri = jnp.asarray(np.tril(np.ones((TILE, TILE), np.float32), -1), dtype=BF16)
    blk = pl.BlockSpec((TILE, LANES), lambda i: (i, 0))
    return pl.pallas_call(
        functools.partial(_route_body, n_exp=n_exp),
        grid=(t // TILE,),
        in_specs=[blk, pl.BlockSpec((TILE, TILE), lambda i: (0, 0))],
        out_specs=[blk, blk, blk, pl.BlockSpec((1, LANES), lambda i: (0, 0))],
        out_shape=[
            jax.ShapeDtypeStruct((t, LANES), I32),
            jax.ShapeDtypeStruct((t, LANES), F32),
            jax.ShapeDtypeStruct((t, LANES), I32),
            jax.ShapeDtypeStruct((1, LANES), F32),
        ],
        scratch_shapes=[pltpu.VMEM((1, LANES), F32)],
        compiler_params=_cparams(("arbitrary",)),
        name="route",
    )(logits, tri)


def _dispatch_body(dest_ref, h_ref, xb_in, xb_out, sem):
    del xb_in
    rows = h_ref.shape[0]

    def row_copy(r, d):
        return pltpu.make_async_copy(h_ref.at[pl.ds(r, 1)], xb_out.at[pl.ds(d, 1)], sem)

    def issue(r, carry):
        for k in range(TOP_K):
            row_copy(r, dest_ref[0, 0, r * TOP_K + k]).start()
        return carry

    def drain(r, carry):
        for k in range(TOP_K):
            row_copy(r, dest_ref[0, 0, r * TOP_K + k]).wait()
        return carry

    lax.fori_loop(0, rows, issue, 0)
    lax.fori_loop(0, rows, drain, 0)


def _dispatch(dest_tiles, h_packed, n_slots):
    t, w = h_packed.shape
    xb = jnp.zeros((n_slots, w), U32)
    return pl.pallas_call(
        _dispatch_body,
        grid=(t // TILE,),
        in_specs=[
            pl.BlockSpec((1, 1, TILE * TOP_K), lambda i: (i, 0, 0), memory_space=pltpu.SMEM),
            pl.BlockSpec((TILE, w), lambda i: (i, 0)),
            pl.BlockSpec(memory_space=pl.ANY),
        ],
        out_specs=pl.BlockSpec(memory_space=pl.ANY),
        out_shape=jax.ShapeDtypeStruct((n_slots, w), U32),
        scratch_shapes=[pltpu.SemaphoreType.DMA(())],
        input_output_aliases={2: 0},
        compiler_params=_cparams(("arbitrary",)),
        name="dispatch",
    )(dest_tiles, h_packed, xb)


def _moe_body(be_ref, bv_ref, nu_ref, xb_ref, wg_ref, wu_ref, wd_ref, bg_ref, bu_ref, bd_ref,
              y_ref, x_scr, wg_scr, wu_scr, wd_scr):
    del be_ref, nu_ref
    i, j = pl.program_id(0), pl.program_id(1)
    valid = bv_ref[i]
    tm = xb_ref.shape[0]
    half_d = xb_ref.shape[1]

    @pl.when(valid > 0)
    def _():
        @pl.when(j == 0)
        def _():
            def unpack(s, carry):
                rs = pl.ds(pl.multiple_of(s * SUB_MOE, SUB_MOE), SUB_MOE)
                hi, lo = _unpack_pair(xb_ref[rs, :])
                x_scr[0, rs, :] = hi
                x_scr[1, rs, :] = lo
                y_ref[rs, :] = jnp.broadcast_to(bd_ref[0, 0], (SUB_MOE, y_ref.shape[1]))
                return carry

            lax.fori_loop(0, tm // SUB_MOE, unpack, 0)

        wg_scr[...] = wg_ref[0, 0].astype(BF16)
        wu_scr[...] = wu_ref[0, 0].astype(BF16)
        wd_scr[...] = wd_ref[0, 0].astype(BF16)

        def body(s, carry):
            rs = pl.ds(pl.multiple_of(s * SUB_MOE, SUB_MOE), SUB_MOE)
            xh, xl = x_scr[0, rs, :], x_scr[1, rs, :]
            g = _dot(xh, wg_scr[0:half_d, :]) + _dot(xl, wg_scr[half_d:, :]) + bg_ref[0, 0]
            u = _dot(xh, wu_scr[0:half_d, :]) + _dot(xl, wu_scr[half_d:, :]) + bu_ref[0, 0]
            g = jnp.minimum(g, SWIGLU_LIMIT)
            u = jnp.clip(u, -SWIGLU_LIMIT, SWIGLU_LIMIT)
            a = g * jax.nn.sigmoid(SWIGLU_ALPHA * g) * (u + 1.0)
            y_ref[rs, :] += _dot(a.astype(BF16), wd_scr[...])
            return carry

        lax.fori_loop(0, (valid + SUB_MOE - 1) // SUB_MOE, body, 0)


def _moe(blk_e, blk_valid, n_used, xb, w_gu, b_gu, w_dn, b_dn, *, layer):
    n_slots, half_d = xb.shape
    d = 2 * half_d
    n_exp, _, two_f = w_gu.shape[1:]
    f = two_f // 2
    tm, tf = TM_MOE, TF_MOE
    nf = f // tf
    nb = n_slots // tm

    def live(i, nu):
        return i < nu[0]

    def row_blk(i, j, be, bv, nu):
        return (jnp.minimum(i, nu[0] - 1), 0)

    def jj(i, j, nu):
        return jnp.where(live(i, nu), j, nf - 1)

    grid_spec = pltpu.PrefetchScalarGridSpec(
        num_scalar_prefetch=3,
        grid=(nb, nf),
        in_specs=[
            pl.BlockSpec((tm, half_d), row_blk),
            pl.BlockSpec((1, 1, d, tf), lambda i, j, be, bv, nu: (layer, be[i], 0, jj(i, j, nu))),
            pl.BlockSpec((1, 1, d, tf), lambda i, j, be, bv, nu: (layer, be[i], 0, nf + jj(i, j, nu))),
            pl.BlockSpec((1, 1, tf, d), lambda i, j, be, bv, nu: (layer, be[i], jj(i, j, nu), 0)),
            pl.BlockSpec((1, 1, 1, tf), lambda i, j, be, bv, nu: (layer, be[i], 0, jj(i, j, nu))),
            pl.BlockSpec((1, 1, 1, tf), lambda i, j, be, bv, nu: (layer, be[i], 0, nf + jj(i, j, nu))),
            pl.BlockSpec((1, 1, 1, d), lambda i, j, be, bv, nu: (layer, be[i], 0, 0)),
        ],
        out_specs=pl.BlockSpec((tm, d), row_blk),
        scratch_shapes=[
            pltpu.VMEM((2, tm, half_d), BF16),
            pltpu.VMEM((d, tf), BF16),
            pltpu.VMEM((d, tf), BF16),
            pltpu.VMEM((tf, d), BF16),
        ],
    )
    nl = w_gu.shape[0]
    return pl.pallas_call(
        _moe_body,
        grid_spec=grid_spec,
        out_shape=jax.ShapeDtypeStruct((n_slots, d), F32),
        compiler_params=_cparams(("arbitrary", "arbitrary")),
        name="moe",
    )(blk_e, blk_valid, n_used, xb, w_gu, w_gu, w_dn,
      b_gu.reshape(nl, n_exp, 1, two_f), b_gu.reshape(nl, n_exp, 1, two_f), b_dn.reshape(nl, n_exp, 1, d))


def _combine_body(dest_ref, gate_ref, x_ref, mod_ref, fg_ref, yb_ref, o_ref, buf, sem, *, final):
    rows = x_ref.shape[0]

    def row_copy(r, k, d):
        return pltpu.make_async_copy(yb_ref.at[pl.ds(d, 1)], buf.at[k, pl.ds(r, 1)], sem)

    def issue(r, carry):
        for k in range(TOP_K):
            row_copy(r, k, dest_ref[0, 0, r * TOP_K + k]).start()
        return carry

    def drain(r, carry):
        for k in range(TOP_K):
            row_copy(r, k, dest_ref[0, 0, r * TOP_K + k]).wait()
        return carry

    lax.fori_loop(0, rows, issue, 0)
    lax.fori_loop(0, rows, drain, 0)

    ff = gate_ref[:, 0:1] * buf[0]
    for k in range(1, TOP_K):
        ff = ff + gate_ref[:, k:k + 1] * buf[k]
    x_new = x_ref[...] + mod_ref[0, 5:6, :] * ff
    if final:
        x_new = _rms_lanes(x_new) * fg_ref[...]
    o_ref[...] = x_new


def _combine(dest_tiles, gates, x, mod, final_g, yb, *, t_ctx, ds, final):
    t, d = x.shape
    grp = functools.partial(_group_of_block, rows=TILE, t_ctx=t_ctx, ds=ds)
    return pl.pallas_call(
        functools.partial(_combine_body, final=final),
        grid=(t // TILE,),
        in_specs=[
            pl.BlockSpec((1, 1, TILE * TOP_K), lambda i: (i, 0, 0), memory_space=pltpu.SMEM),
            pl.BlockSpec((TILE, LANES), lambda i: (i, 0)),
            pl.BlockSpec((TILE, d), lambda i: (i, 0)),
            pl.BlockSpec((1, 6, d), lambda i: (grp(i), 0, 0)),
            pl.BlockSpec((1, d), lambda i: (0, 0)),
            pl.BlockSpec(memory_space=pl.ANY),
        ],
        out_specs=pl.BlockSpec((TILE, d), lambda i: (i, 0)),
        out_shape=jax.ShapeDtypeStruct((t, d), F32),
        scratch_shapes=[pltpu.VMEM((TOP_K, TILE, d), F32), pltpu.SemaphoreType.DMA(())],
        compiler_params=_cparams(("arbitrary",)),
        name="combine",
    )(dest_tiles, gates, x, mod, final_g, yb)


def _slot_layout(idx, rank, counts, n_blocks):
    n_exp = counts.shape[0]
    padded = (counts + TM_MOE - 1) // TM_MOE * TM_MOE
    pend = jnp.cumsum(padded)
    pstart = pend - padded
    dest = pstart[idx] + rank
    bstart = jnp.arange(n_blocks, dtype=I32) * TM_MOE
    blk_e = jnp.minimum(jnp.searchsorted(pend, bstart, side='right'), n_exp - 1).astype(I32)
    blk_valid = jnp.clip(pstart[blk_e] + counts[blk_e] - bstart, 0, TM_MOE).astype(I32)
    n_used = (pend[-1] // TM_MOE).astype(I32).reshape(1)
    blk_e = jnp.where(bstart < pend[-1], blk_e, blk_e[jnp.maximum(n_used[0] - 1, 0)])
    return dest.astype(I32), blk_e, blk_valid, n_used


def kernel(x_prompt, x_sample, cache_k, cache_v, c, c_ctx, norm1_g, norm2_g, w_ada, b_ada, w_in, w_out,
           lam_q1, lam_k1, lam_q2, lam_k2, diff_ln_g, gmlp_norm_g, gmlp_ws, gmlp_bs, pool_w, pool_scale,
           router_w, router_b, exp_w_gu, exp_b_gu, exp_w_down, exp_b_down, final_g):
    nb_ctx, s_ctx, d = x_prompt.shape
    nb_lat, ds, _ = x_sample.shape
    depth = w_in.shape[0]
    n_exp = router_w.shape[-1]
    p_len = cache_k.shape[2]
    t_ctx, t_lat = nb_ctx * s_ctx, nb_lat * ds
    t = t_ctx + t_lat
    assert t_ctx % TILE == 0 and t_lat % TILE == 0 and (t * TOP_K) % TM_MOE == 0

    x = jnp.concatenate([x_prompt.reshape(t_ctx, d), x_sample.reshape(t_lat, d)], axis=0)
    cond_t = jnp.concatenate([c_ctx[None, :], c], axis=0).T
    mod = _ada(cond_t, w_ada, b_ada)
    mod = mod.reshape(depth, 1 + nb_lat, 6, d)

    rope_tabs = _rope_tables(ds)
    ck = cache_k.reshape(nb_lat, depth, p_len, A_HEADS * HEAD_W)
    cv = cache_v.reshape(nb_lat, depth, p_len, A_WIDTH)
    w_in_b = w_in.astype(BF16)
    w_out_b = w_out.astype(BF16)
    ws_b = gmlp_ws.astype(BF16)
    pw_b = pool_w.astype(BF16)
    rw_pad = jnp.pad(router_w, ((0, 0), (0, 0), (0, LANES - n_exp)))
    rb_pad = jnp.pad(router_b, ((0, 0), (0, LANES - n_exp)))
    n_blocks = t * TOP_K // TM_MOE + n_exp
    n_slots = n_blocks * TM_MOE

    new_k, new_v = [], []
    for l in range(depth):
        lam_init = 0.8 - 0.6 * math.exp(-0.3 * l)
        lamp = jnp.stack([lam_q1[l], lam_k1[l], lam_q2[l], lam_k2[l]])
        lng = diff_ln_g[l][None, :]
        z = _inproj(x, mod[l], norm1_g[l][None, :], w_in_b[l], t_ctx=t_ctx, ds=ds)
        new_k.append(z[:t_ctx, Q_W:2 * Q_W].reshape(nb_ctx, s_ctx, A_HEADS, HEAD_W))
        new_v.append(z[:t_ctx, 2 * Q_W:2 * Q_W + A_WIDTH].reshape(nb_ctx, s_ctx, A_HEADS, V_DIM))
        oa_ctx = _attn_ctx(z, lamp, lng, nb=nb_ctx, s=s_ctx, lam_init=lam_init)
        oa_lat = _attn_lat(z, ck, cv, rope_tabs, lamp, lng, layer=l, t_ctx=t_ctx, nb=nb_lat, ds=ds,
                           lam_init=lam_init)
        oa = jnp.concatenate([oa_ctx, oa_lat], axis=0)
        gb = jnp.repeat(gmlp_bs[l].T, B_CH, axis=1)
        x, h_packed, logits = _mixpost(
            oa, z, x, mod[l], norm2_g[l][None, :], w_out_b[l], gmlp_norm_g[l][None, :], ws_b[l], gb,
            pw_b[l], pool_scale[l][None, :], rw_pad[l], rb_pad[l][None, :],
            t_ctx=t_ctx, s_ctx=s_ctx, ds=ds)
        idx, gates, rank, counts = _route(logits, n_exp)
        dest, blk_e, blk_valid, n_used = _slot_layout(
            idx[:, :TOP_K], rank[:, :TOP_K], counts[0, :n_exp].astype(I32), n_blocks)
        dest_tiles = dest.reshape(t // TILE, 1, TILE * TOP_K)
        xb = _dispatch(dest_tiles, h_packed, n_slots)
        yb = _moe(blk_e, blk_valid, n_used, xb, exp_w_gu, exp_b_gu, exp_w_down, exp_b_down, layer=l)
        x = _combine(dest_tiles, gates, x, mod[l], final_g[None, :], yb, t_ctx=t_ctx, ds=ds,
                     final=(l == depth - 1))

    y_prompt = x[:t_ctx].reshape(nb_ctx, s_ctx, d)
    y_sample = x[t_ctx:].reshape(nb_lat, ds, d)
    return y_prompt, y_sample, jnp.stack(new_k, axis=1), jnp.stack(new_v, axis=1)
```

```python
import functools
import math

import numpy as np
import jax
import jax.numpy as jnp
from jax import lax
from jax.experimental import pallas as pl
from jax.experimental.pallas import tpu as pltpu

F32 = jnp.float32
BF16 = jnp.bfloat16
I32 = jnp.int32
U32 = jnp.uint32

GRID_W = 64
A_HEADS = 8
QK_DIM = 64
V_DIM = 2 * QK_DIM
HEAD_W = 2 * QK_DIM
Q_W = A_HEADS * HEAD_W
A_WIDTH = A_HEADS * V_DIM
B_GROUPS = 4
CHUNK = 128
B_CH = 128
B_WIDTH = B_GROUPS * B_CH
POOL_WINDOWS = (2, 4, 8, 16)
N_POOL = len(POOL_WINDOWS)
C_CH = 128
C_WIDTH = N_POOL * C_CH
ROPE_BASE = 10000.0
N_FREQ = QK_DIM // 4
TOP_K = 4
SWIGLU_ALPHA = 1.702
SWIGLU_LIMIT = 7.0
EPS = 1e-6
QK_SCALE = QK_DIM ** -0.5

LANES = 128
TILE = 256
HALO = 16
TM_PROJ_CHOICES = (1024, 512, 256)
LOG2E = 1.4426950408889634
TM_MOE = 1024
SUB_MOE = 256
TF_MOE = 256
TQ_ATT = 512
VMEM_LIMIT = 56 * 1024 * 1024


def _cparams(sem, vmem=VMEM_LIMIT):
    return pltpu.CompilerParams(dimension_semantics=sem, vmem_limit_bytes=vmem)


def _norm_mod(x, g, shift, scale):
    ms = jnp.mean(x * x, axis=-1, keepdims=True)
    return x * lax.rsqrt(ms + EPS) * g * (1.0 + scale) + shift


def _rms_lanes(x):
    return x * lax.rsqrt(jnp.mean(x * x, axis=-1, keepdims=True) + EPS)


def _dot(a, b):
    return jnp.dot(a, b, preferred_element_type=F32)


def _dot_nt(a, b):
    return lax.dot_general(a, b, (((1,), (1,)), ((), ())), preferred_element_type=F32)


def _ada_body(ct_ref, w_ref, b_ref, o_ref, *, kc):
    d, ng = ct_ref.shape
    tn = w_ref.shape[-1]

    def body(c, accs):
        k0 = pl.multiple_of(c * kc, kc)
        w = w_ref[0, pl.ds(k0, kc), :]
        s = jax.nn.silu(ct_ref[pl.ds(k0, kc), :])
        return tuple(
            acc + jnp.sum((w * s[:, gi:gi + 1]).reshape(kc // 8, 8, tn), axis=0)
            for gi, acc in enumerate(accs))

    accs = lax.fori_loop(0, d // kc, body, tuple(jnp.zeros((8, tn), F32) for _ in range(ng)))
    for gi in range(ng):
        o_ref[0, gi:gi + 1, :] = jnp.sum(accs[gi], axis=0, keepdims=True) + b_ref[0]


def _ada(cond_t, w_ada, b_ada):
    d, ng = cond_t.shape
    nl, _, n = w_ada.shape
    tn = 1024
    return pl.pallas_call(
        functools.partial(_ada_body, kc=32),
        grid=(nl, n // tn),
        in_specs=[
            pl.BlockSpec((d, ng), lambda l, j: (0, 0)),
            pl.BlockSpec((1, d, tn), lambda l, j: (l, 0, j)),
            pl.BlockSpec((1, 1, tn), lambda l, j: (l, 0, j)),
        ],
        out_specs=pl.BlockSpec((1, ng, tn), lambda l, j: (l, 0, j)),
        out_shape=jax.ShapeDtypeStruct((nl, ng, n), F32),
        compiler_params=_cparams(("parallel", "parallel")),
        name="ada",
    )(cond_t, w_ada, b_ada.reshape(nl, 1, n))


def _inproj_body(xc_ref, xl_ref, mod_ref, g_ref, w_ref, z_ref, h_scr, *, rc, n_ctx_blocks):
    @pl.when(pl.program_id(1) == 0)
    def _():
        shift = mod_ref[0, 0:1, :]
        scale = mod_ref[0, 1:2, :]
        g = g_ref[...]

        def fill(x_ref):
            def body(c, carry):
                r0 = pl.multiple_of(c * rc, rc)
                h = _norm_mod(x_ref[pl.ds(r0, rc), :], g, shift, scale)
                h_scr[pl.ds(r0, rc), :] = h.astype(BF16)
                return carry

            lax.fori_loop(0, x_ref.shape[0] // rc, body, 0)

        is_ctx = pl.program_id(0) < n_ctx_blocks
        pl.when(is_ctx)(lambda: fill(xc_ref))
        pl.when(jnp.logical_not(is_ctx))(lambda: fill(xl_ref))

    z_ref[...] = _dot(h_scr[...], w_ref[...])


def _group_of_block(i, rows, t_ctx, ds):
    r = i * rows
    return jnp.where(r < t_ctx, 0, 1 + (r - t_ctx) // ds)


def _pair_specs(rows, width, n_ctx_blocks):
    return [
        pl.BlockSpec((rows, width), lambda i, *_: (jnp.minimum(i, n_ctx_blocks - 1), 0)),
        pl.BlockSpec((rows, width), lambda i, *_: (jnp.maximum(i - n_ctx_blocks, 0), 0)),
    ]


def _inproj(x_ctx, x_lat, mod, g1, w_in, *, ds):
    t_ctx, d = x_ctx.shape
    t = t_ctx + x_lat.shape[0]
    n = w_in.shape[1]
    tm = next(m for m in TM_PROJ_CHOICES if t_ctx % m == 0 and ds % m == 0)
    tn = 768
    assert n % tn == 0
    grp = functools.partial(_group_of_block, rows=tm, t_ctx=t_ctx, ds=ds)
    return pl.pallas_call(
        functools.partial(_inproj_body, rc=64, n_ctx_blocks=t_ctx // tm),
        grid=(t // tm, n // tn),
        in_specs=_pair_specs(tm, d, t_ctx // tm) + [
            pl.BlockSpec((1, 6, d), lambda i, j: (grp(i), 0, 0)),
            pl.BlockSpec((1, d), lambda i, j: (0, 0)),
            pl.BlockSpec((d, tn), lambda i, j: (0, j)),
        ],
        out_specs=pl.BlockSpec((tm, tn), lambda i, j: (i, j)),
        out_shape=jax.ShapeDtypeStruct((t, n), F32),
        scratch_shapes=[pltpu.VMEM((tm, d), BF16)],
        compiler_params=_cparams(("arbitrary", "arbitrary")),
        name="inproj",
    )(x_ctx, x_lat, mod, g1, w_in)


def _lam_value(lamp_ref, lam_init):
    lp = lamp_ref[...]
    a = jnp.sum(lp[0:1] * lp[1:2], keepdims=True)
    b = jnp.sum(lp[2:3] * lp[3:4], keepdims=True)
    return jnp.exp(a) - jnp.exp(b) + lam_init


def _split_halves(q):
    first = lax.broadcasted_iota(I32, (1, HEAD_W), 1) < QK_DIM
    return jnp.where(first, q, 0.0).astype(BF16), jnp.where(first, 0.0, q).astype(BF16)


def _attn_ctx_body(lamp_ref, lng_ref, q_ref, k_ref, v_ref, o_ref, ko_ref, vo_ref, *, lam_init):
    lam = _lam_value(lamp_ref, lam_init)
    g = lng_ref[...] * (1.0 - lam_init)
    ko_ref[...] = k_ref[...]
    vo_ref[...] = v_ref[...]
    for h in range(A_HEADS):
        hs = slice(h * HEAD_W, (h + 1) * HEAD_W)
        k = k_ref[:, hs].astype(BF16)
        v = v_ref[:, hs].astype(BF16)
        ps = []
        for qh in _split_halves(q_ref[:, hs] * QK_SCALE):
            s = _dot_nt(qh, k)
            e = jnp.exp(s - jnp.max(s, axis=-1, keepdims=True))
            ps.append(e * (1.0 / jnp.sum(e, axis=-1, keepdims=True)))
        o = _dot((ps[0] - lam * ps[1]).astype(BF16), v)
        o_ref[:, hs] = (_rms_lanes(o) * g).astype(o_ref.dtype)


def _attn_ctx(z, lamp, lng, *, nb, s, lam_init):
    return pl.pallas_call(
        functools.partial(_attn_ctx_body, lam_init=lam_init),
        grid=(nb,),
        in_specs=[
            pl.BlockSpec(lamp.shape, lambda b: (0, 0)),
            pl.BlockSpec((1, V_DIM), lambda b: (0, 0)),
            pl.BlockSpec((s, Q_W), lambda b: (b, 0)),
            pl.BlockSpec((s, Q_W), lambda b: (b, 1)),
            pl.BlockSpec((s, A_WIDTH), lambda b: (b, 2)),
        ],
        out_specs=[pl.BlockSpec((s, A_WIDTH), lambda b: (b, 0))] * 3,
        out_shape=[
            jax.ShapeDtypeStruct((nb * s, A_WIDTH), BF16),
            jax.ShapeDtypeStruct((nb * s, Q_W), F32),
            jax.ShapeDtypeStruct((nb * s, A_WIDTH), F32),
        ],
        compiler_params=_cparams(("parallel",)),
        name="attn_ctx",
    )(lamp, lng, z, z, z)


def _rope(x, c, sa, sb):
    return x * c + pltpu.roll(x, HEAD_W - N_FREQ, 1) * sa + pltpu.roll(x, N_FREQ, 1) * sb


def _attn_lat_body(lamp_ref, lng_ref, q_ref, k_ref, v_ref, ck_ref, cv_ref,
                   cq_ref, saq_ref, sbq_ref, ckk_ref, sak_ref, sbk_ref,
                   o_ref, kall, vall, s_scr, e_scr, *, lam_init, kc, rc):
    ds = k_ref.shape[0]
    p = ck_ref.shape[2]
    tq = q_ref.shape[0]
    nch = (ds + p) // kc

    @pl.when(pl.program_id(2) == 0)
    def _():
        def ones_col(rows):
            return jnp.where(lax.broadcasted_iota(I32, (rows, LANES), 1) == 0, 1.0, 0.0).astype(BF16)

        def body(c, carry):
            r0 = pl.multiple_of(c * rc, rc)
            rs = pl.ds(r0, rc)
            kall[rs, :] = _rope(k_ref[rs, :], ckk_ref[rs, :], sak_ref[rs, :], sbk_ref[rs, :]).astype(BF16)
            vall[rs, 0:V_DIM] = v_ref[rs, :].astype(BF16)
            vall[rs, V_DIM:] = ones_col(rc)
            return carry

        lax.fori_loop(0, ds // rc, body, 0)
        kall[ds:ds + p, :] = ck_ref[0, 0].astype(BF16)
        vall[ds:ds + p, 0:V_DIM] = cv_ref[0, 0].astype(BF16)
        vall[ds:ds + p, V_DIM:] = ones_col(p)

    lam = _lam_value(lamp_ref, lam_init)
    q = _rope(q_ref[...], cq_ref[...], saq_ref[...], sbq_ref[...]) * (QK_SCALE * LOG2E)
    halves = _split_halves(q)
    chunks = [slice(c * kc, (c + 1) * kc) for c in range(nch)]

    def qk(n, ch, mrun):
        s = _dot_nt(halves[n], kall[ch, :])
        s_scr[n, :, ch] = s
        for t in range(kc // LANES):
            mrun = jnp.maximum(mrun, s[:, t * LANES:(t + 1) * LANES])
        return mrun

    def ex(n, ch, m):
        e_scr[n, :, ch] = jnp.exp2(s_scr[n, :, ch] - m).astype(BF16)

    neg = jnp.full((tq, LANES), -jnp.inf, F32)
    mrun = neg
    for ch in chunks:
        mrun = qk(0, ch, mrun)
    m0 = jnp.max(mrun, axis=-1, keepdims=True)
    mrun = neg
    for ch in chunks:
        mrun = qk(1, ch, mrun)
        ex(0, ch, m0)
    m1 = jnp.max(mrun, axis=-1, keepdims=True)
    oe0 = jnp.zeros((tq, V_DIM + LANES), F32)
    for ch in chunks:
        ex(1, ch, m1)
        oe0 = oe0 + _dot(e_scr[0, :, ch], vall[ch, :])
    oe1 = _dot(e_scr[1], vall[...])
    outs = [oe[:, 0:V_DIM] * (1.0 / oe[:, V_DIM:V_DIM + 1]) for oe in (oe0, oe1)]
    o = outs[0] - lam * outs[1]
    o_ref[...] = (_rms_lanes(o) * (lng_ref[...] * (1.0 - lam_init))).astype(o_ref.dtype)


def _attn_lat(z, cache_k, cache_v, rope_tabs, lamp, lng, *, layer, t_ctx, nb, ds, lam_init):
    p = cache_k.shape[2]
    tq = min(TQ_ATT, ds)
    sk = ds + p
    kc = 512 if sk % 512 == 0 else LANES
    assert t_ctx % ds == 0 and ds % tq == 0 and sk % kc == 0
    seq0, q0 = t_ctx // ds, t_ctx // tq
    nq = ds // tq
    qh, kh, vh = 0, Q_W // HEAD_W, 2 * Q_W // HEAD_W
    tab_q = pl.BlockSpec((tq, HEAD_W), lambda b, h, qi: (qi, 0))
    tab_k = pl.BlockSpec((ds, HEAD_W), lambda b, h, qi: (0, 0))
    cos, sa, sb = rope_tabs
    return pl.pallas_call(
        functools.partial(_attn_lat_body, lam_init=lam_init, kc=kc, rc=256),
        grid=(nb, A_HEADS, nq),
        in_specs=[
            pl.BlockSpec(lamp.shape, lambda b, h, qi: (0, 0)),
            pl.BlockSpec((1, V_DIM), lambda b, h, qi: (0, 0)),
            pl.BlockSpec((tq, HEAD_W), lambda b, h, qi: (q0 + b * nq + qi, qh + h)),
            pl.BlockSpec((ds, HEAD_W), lambda b, h, qi: (seq0 + b, kh + h)),
            pl.BlockSpec((ds, HEAD_W), lambda b, h, qi: (seq0 + b, vh + h)),
            pl.BlockSpec((1, 1, p, HEAD_W), lambda b, h, qi: (b, layer, 0, h)),
            pl.BlockSpec((1, 1, p, V_DIM), lambda b, h, qi: (b, layer, 0, h)),
            tab_q, tab_q, tab_q, tab_k, tab_k, tab_k,
        ],
        out_specs=pl.BlockSpec((tq, V_DIM), lambda b, h, qi: (b * nq + qi, h)),
        out_shape=jax.ShapeDtypeStruct((nb * ds, A_WIDTH), BF16),
        scratch_shapes=[
            pltpu.VMEM((sk, HEAD_W), BF16),
            pltpu.VMEM((sk, V_DIM + LANES), BF16),
            pltpu.VMEM((2, tq, sk), F32),
            pltpu.VMEM((2, tq, sk), BF16),
        ],
        compiler_params=_cparams(("parallel", "parallel", "arbitrary")),
        name="attn_lat",
    )(lamp, lng, z, z, z, cache_k, cache_v, cos, sa, sb, cos, sa, sb)


def _rope_tables(n_tok):
    n_rows = n_tok // GRID_W
    row = jnp.repeat(jnp.arange(n_rows), GRID_W).astype(F32)
    col = jnp.tile(jnp.arange(GRID_W), n_rows).astype(F32)
    inv = 1.0 / (ROPE_BASE ** (jnp.arange(N_FREQ, dtype=F32) / N_FREQ))
    ang = jnp.stack([row[:, None] * inv, col[:, None] * inv], axis=1)
    cos, sin = jnp.cos(ang), jnp.sin(ang)
    zero = jnp.zeros_like(sin)

    def lanes(first, second):
        per_axis = jnp.concatenate([first, second], axis=-1)
        return jnp.tile(per_axis.reshape(n_tok, 2 * 2 * N_FREQ), (1, 2))

    return lanes(cos, cos), lanes(-sin, zero), lanes(zero, sin)


def _pack_pair(hi, lo):
    hb = lax.bitcast_convert_type(hi.astype(BF16).astype(F32), U32)
    lb = lax.bitcast_convert_type(lo.astype(BF16).astype(F32), U32)
    return hb | (lb >> 16)


def _unpack_pair(u):
    hi = lax.bitcast_convert_type(u & jnp.uint32(0xFFFF0000), F32)
    lo = lax.bitcast_convert_type(u << 16, F32)
    return hi.astype(BF16), lo.astype(BF16)


def _hi_lo(x):
    hi = x.astype(BF16)
    return hi, (x - hi.astype(F32)).astype(BF16)


def _mixpost_body(oac_ref, oal_ref, zb_ref, zc_ref, zp_ref, zn_ref, xc_ref, xl_ref, mod_ref, g2_ref,
                  wout_ref, gng_ref, ws_ref, gb_ref, pw_ref, ps_ref, band_ref, bandp_ref, bandn_ref,
                  rw_ref, rb_ref, xo_ref, h_ref, lg_ref, cat_scr, *, n_ctx_tiles, tps_ctx, tps_lat):
    i = pl.program_id(0)
    is_ctx = i < n_ctx_tiles
    pos = jnp.where(is_ctx, i % tps_ctx, (i - n_ctx_tiles) % tps_lat)
    first = pos == 0
    last = pos == jnp.where(is_ctx, tps_ctx, tps_lat) - 1

    cat_scr[:, 0:A_WIDTH] = jnp.where(is_ctx, oac_ref[...], oal_ref[...])

    zb = jax.nn.gelu(zb_ref[...])
    u, v = zb[:, :B_WIDTH], zb[:, B_WIDTH:]
    for g in range(B_GROUPS):
        gs = slice(g * B_CH, (g + 1) * B_CH)
        vn = (_rms_lanes(v[:, gs]) * gng_ref[:, gs]).astype(BF16)
        for n in range(TILE // CHUNK):
            rs = slice(n * CHUNK, (n + 1) * CHUNK)
            mixed = _dot(ws_ref[g], vn[rs]) + gb_ref[:, gs]
            cat_scr[rs, A_WIDTH + g * B_CH:A_WIDTH + (g + 1) * B_CH] = (u[rs, gs] * mixed).astype(BF16)

    xc = zc_ref[...]
    xc_parts = _hi_lo(xc)
    xp_parts = _hi_lo(zp_ref[...] * jnp.where(first, 0.0, 1.0))
    xn_parts = _hi_lo(zn_ref[...] * jnp.where(last, 0.0, 1.0))
    r = lax.broadcasted_iota(I32, (TILE, 1), 0)
    for g, w in enumerate(POOL_WINDOWS):
        gs = slice(g * C_CH, (g + 1) * C_CH)
        half = w // 2
        acc = jnp.zeros((TILE, C_CH), F32)
        for part in xc_parts:
            acc = acc + _dot(band_ref[g], part[:, gs])
        for part in xp_parts:
            acc = acc + _dot(bandp_ref[g], part[:, gs])
        for part in xn_parts:
            acc = acc + _dot(bandn_ref[g], part[:, gs])
        left = jnp.where(first, jnp.minimum(half, r), half)
        right = jnp.where(last, jnp.minimum(half - 1, TILE - 1 - r), half - 1)
        cnt = (left + right + 1).astype(F32)
        pooled = acc / cnt - xc[:, gs]
        y = _dot(pooled.astype(BF16), pw_ref[g]) * ps_ref[:, gs]
        c0 = A_WIDTH + B_WIDTH + g * C_CH
        cat_scr[:, c0:c0 + C_CH] = y.astype(BF16)

    mix = _dot(cat_scr[...], wout_ref[...])
    x_new = jnp.where(is_ctx, xc_ref[...], xl_ref[...]) + mod_ref[0, 2:3, :] * mix
    xo_ref[...] = x_new
    h2 = _norm_mod(x_new, g2_ref[...], mod_ref[0, 3:4, :], mod_ref[0, 4:5, :])
    half_d = h2.shape[1] // 2
    h_ref[...] = _pack_pair(h2[:, :half_d], h2[:, half_d:])
    h_hi, h_lo = _hi_lo(h2)
    lg_ref[...] = _dot(h_hi, rw_ref[0]) + (_dot(h_lo, rw_ref[0]) + _dot(h_hi, rw_ref[1])) + rb_ref[...]


def _band_matrices():
    r = np.arange(TILE)[:, None]
    cur, prev, nxt = [], [], []
    for w in POOL_WINDOWS:
        half = w // 2
        c = np.arange(TILE)[None, :]
        cur.append((c >= r - half) & (c <= r + half - 1))
        ch = np.arange(HALO)[None, :]
        prev.append(ch - HALO >= r - half)
        nxt.append(TILE + ch <= r + half - 1)
    to = lambda m: jnp.asarray(np.stack(m).astype(np.float32), dtype=BF16)
    return to(cur), to(prev), to(nxt)


def _mixpost(oa_ctx, oa_lat, z, x_ctx, x_lat, mod, g2, w_out, gng, ws, gb, pw, ps, rw, rb, *, s_ctx, ds):
    t_ctx, d = x_ctx.shape
    t = t_ctx + x_lat.shape[0]
    nt = t // TILE
    n_ctx_tiles = t_ctx // TILE
    tps_ctx, tps_lat = s_ctx // TILE, ds // TILE
    assert s_ctx % TILE == 0 and ds % TILE == 0
    grp = functools.partial(_group_of_block, rows=TILE, t_ctx=t_ctx, ds=ds)
    band, bandp, bandn = _band_matrices()
    zb_blk = (2 * Q_W + A_WIDTH) // (2 * B_WIDTH)
    zc_blk = (2 * Q_W + A_WIDTH + 2 * B_WIDTH) // C_WIDTH
    hpt = TILE // HALO
    const2 = lambda i: (0, 0)
    const3 = lambda i: (0, 0, 0)
    return pl.pallas_call(
        functools.partial(_mixpost_body, n_ctx_tiles=n_ctx_tiles, tps_ctx=tps_ctx, tps_lat=tps_lat),
        grid=(nt,),
        in_specs=_pair_specs(TILE, A_WIDTH, n_ctx_tiles) + [
            pl.BlockSpec((TILE, 2 * B_WIDTH), lambda i: (i, zb_blk)),
            pl.BlockSpec((TILE, C_WIDTH), lambda i: (i, zc_blk)),
            pl.BlockSpec((HALO, C_WIDTH), lambda i: (jnp.maximum(i * hpt - 1, 0), zc_blk)),
            pl.BlockSpec((HALO, C_WIDTH), lambda i: (jnp.minimum((i + 1) * hpt, nt * hpt - 1), zc_blk)),
        ] + _pair_specs(TILE, d, n_ctx_tiles) + [
            pl.BlockSpec((1, 6, d), lambda i: (grp(i), 0, 0)),
            pl.BlockSpec((1, d), const2),
            pl.BlockSpec(w_out.shape, const2),
            pl.BlockSpec((1, B_WIDTH), const2),
            pl.BlockSpec(ws.shape, const3),
            pl.BlockSpec(gb.shape, const2),
            pl.BlockSpec(pw.shape, const3),
            pl.BlockSpec((1, C_WIDTH), const2),
            pl.BlockSpec(band.shape, const3),
            pl.BlockSpec(bandp.shape, const3),
            pl.BlockSpec(bandn.shape, const3),
            pl.BlockSpec(rw.shape, const3),
            pl.BlockSpec(rb.shape, const2),
        ],
        out_specs=[
            pl.BlockSpec((TILE, d), lambda i: (i, 0)),
            pl.BlockSpec((TILE, d // 2), lambda i: (i, 0)),
            pl.BlockSpec((TILE, LANES), lambda i: (i, 0)),
        ],
        out_shape=[
            jax.ShapeDtypeStruct((t, d), F32),
            jax.ShapeDtypeStruct((t, d // 2), U32),
            jax.ShapeDtypeStruct((t, LANES), F32),
        ],
        scratch_shapes=[pltpu.VMEM((TILE, d), BF16)],
        compiler_params=_cparams(("arbitrary",)),
        name="mixpost",
    )(oa_ctx, oa_lat, z, z, z, z, x_ctx, x_lat, mod, g2, w_out, gng, ws, gb, pw, ps, band, bandp, bandn, rw, rb)


def _route_body(lg_ref, tri_ref, idx_ref, gate_ref, rank_ref, cnt_ref, carry, *, n_exp):
    @pl.when(pl.program_id(0) == 0)
    def _():
        carry[...] = jnp.zeros_like(carry)

    lane = lax.broadcasted_iota(I32, lg_ref.shape, 1).astype(F32)
    l = jnp.where(lane < n_exp, lg_ref[...], -jnp.inf)
    vals, idxs, sels = [], [], []
    for _ in range(TOP_K):
        m = jnp.max(l, axis=-1, keepdims=True)
        idx = jnp.min(jnp.where(l == m, lane, float(LANES)), axis=-1, keepdims=True)
        sel = lane == idx
        vals.append(m)
        idxs.append(idx)
        sels.append(sel)
        l = jnp.where(sel, -jnp.inf, l)
    chosen = functools.reduce(jnp.logical_or, sels)
    chosen_f = jnp.where(chosen, 1.0, 0.0)
    prefix = carry[...] + _dot(tri_ref[...], chosen_f.astype(BF16))
    es = [jnp.exp(v - vals[0]) for v in vals]
    tot = functools.reduce(jnp.add, es)
    idx_o = jnp.zeros(lg_ref.shape, F32)
    gate_o = jnp.zeros(lg_ref.shape, F32)
    rank_o = jnp.zeros(lg_ref.shape, F32)
    for k in range(TOP_K):
        rank_k = jnp.sum(jnp.where(sels[k], prefix, 0.0), axis=-1, keepdims=True)
        idx_o = jnp.where(lane == k, idxs[k], idx_o)
        gate_o = jnp.where(lane == k, es[k] / tot, gate_o)
        rank_o = jnp.where(lane == k, rank_k, rank_o)
    idx_ref[...] = idx_o.astype(I32)
    gate_ref[...] = gate_o
    rank_ref[...] = rank_o.astype(I32)
    carry[...] = carry[...] + jnp.sum(chosen_f, axis=0, keepdims=True)
    cnt_ref[...] = carry[...]


def _route(logits, n_exp):
    t = logits.shape[0]
    tri = jnp.asarray(np.tril(np.ones((TILE, TILE), np.float32), -1), dtype=BF16)
    blk = pl.BlockSpec((TILE, LANES), lambda i: (i, 0))
    return pl.pallas_call(
        functools.partial(_route_body, n_exp=n_exp),
        grid=(t // TILE,),
        in_specs=[blk, pl.BlockSpec((TILE, TILE), lambda i: (0, 0))],
        out_specs=[blk, blk, blk, pl.BlockSpec((1, LANES), lambda i: (0, 0))],
        out_shape=[
            jax.ShapeDtypeStruct((t, LANES), I32),
            jax.ShapeDtypeStruct((t, LANES), F32),
            jax.ShapeDtypeStruct((t, LANES), I32),
            jax.ShapeDtypeStruct((1, LANES), F32),
        ],
        scratch_shapes=[pltpu.VMEM((1, LANES), F32)],
        compiler_params=_cparams(("arbitrary",)),
        name="route",
    )(logits, tri)


def _dispatch_body(dest_ref, h_ref, xb_in, xb_out, sem):
    del xb_in
    rows = h_ref.shape[0]

    def row_copy(r, d):
        return pltpu.make_async_copy(h_ref.at[pl.ds(r, 1)], xb_out.at[pl.ds(d, 1)], sem)

    def issue(r, carry):
        for k in range(TOP_K):
            row_copy(r, dest_ref[0, 0, r * TOP_K + k]).start()
        return carry

    def drain(r, carry):
        for k in range(TOP_K):
            row_copy(r, dest_ref[0, 0, r * TOP_K + k]).wait()
        return carry

    lax.fori_loop(0, rows, issue, 0)
    lax.fori_loop(0, rows, drain, 0)


def _dispatch(dest_tiles, h_packed, n_slots):
    t, w = h_packed.shape
    xb = jnp.zeros((n_slots, w), U32)
    return pl.pallas_call(
        _dispatch_body,
        grid=(t // TILE,),
        in_specs=[
            pl.BlockSpec((1, 1, TILE * TOP_K), lambda i: (i, 0, 0), memory_space=pltpu.SMEM),
            pl.BlockSpec((TILE, w), lambda i: (i, 0)),
            pl.BlockSpec(memory_space=pl.ANY),
        ],
        out_specs=pl.BlockSpec(memory_space=pl.ANY),
        out_shape=jax.ShapeDtypeStruct((n_slots, w), U32),
        scratch_shapes=[pltpu.SemaphoreType.DMA(())],
        input_output_aliases={2: 0},
        compiler_params=_cparams(("arbitrary",)),
        name="dispatch",
    )(dest_tiles, h_packed, xb)


def _moe_body(be_ref, bv_ref, nu_ref, xb_ref, wg_ref, wu_ref, wd_ref, bg_ref, bu_ref, bd_ref,
              y_ref, x_scr):
    del be_ref, nu_ref
    i, j = pl.program_id(0), pl.program_id(1)
    valid = bv_ref[i]
    tm = xb_ref.shape[0]
    half_d = xb_ref.shape[1]

    @pl.when(valid > 0)
    def _():
        @pl.when(j == 0)
        def _():
            def unpack(s, carry):
                rs = pl.ds(pl.multiple_of(s * SUB_MOE, SUB_MOE), SUB_MOE)
                hi, lo = _unpack_pair(xb_ref[rs, :])
                x_scr[rs, 0:half_d] = hi
                x_scr[rs, half_d:] = lo
                y_ref[rs, :] = jnp.broadcast_to(bd_ref[0, 0], (SUB_MOE, y_ref.shape[1]))
                return carry

            lax.fori_loop(0, tm // SUB_MOE, unpack, 0)

        def ffn(rows):
            x = x_scr[0:rows, :]
            g = _dot(x, wg_ref[0, 0].astype(BF16)) + bg_ref[0, 0]
            u = _dot(x, wu_ref[0, 0].astype(BF16)) + bu_ref[0, 0]
            g = jnp.minimum(g, SWIGLU_LIMIT)
            u = jnp.clip(u, -SWIGLU_LIMIT, SWIGLU_LIMIT)
            a = g * jax.nn.sigmoid(SWIGLU_ALPHA * g) * (u + 1.0)
            y_ref[0:rows, :] += _dot(a.astype(BF16), wd_ref[0, 0].astype(BF16))

        n_sub = (valid + SUB_MOE - 1) // SUB_MOE
        for k in range(1, tm // SUB_MOE + 1):
            pl.when(n_sub == k)(functools.partial(ffn, k * SUB_MOE))


def _moe(blk_e, blk_valid, n_used, xb, w_gu, b_gu, w_dn, b_dn, *, layer):
    n_slots, half_d = xb.shape
    d = 2 * half_d
    n_exp, _, two_f = w_gu.shape[1:]
    f = two_f // 2
    tm, tf = TM_MOE, TF_MOE
    nf = f // tf
    nb = n_slots // tm

    def live(i, nu):
        return i < nu[0]

    def row_blk(i, j, be, bv, nu):
        return (jnp.minimum(i, nu[0] - 1), 0)

    def jj(i, j, nu):
        return jnp.where(live(i, nu), j, nf - 1)

    grid_spec = pltpu.PrefetchScalarGridSpec(
        num_scalar_prefetch=3,
        grid=(nb, nf),
        in_specs=[
            pl.BlockSpec((tm, half_d), row_blk),
            pl.BlockSpec((1, 1, d, tf), lambda i, j, be, bv, nu: (layer, be[i], 0, jj(i, j, nu))),
            pl.BlockSpec((1, 1, d, tf), lambda i, j, be, bv, nu: (layer, be[i], 0, nf + jj(i, j, nu))),
            pl.BlockSpec((1, 1, tf, d), lambda i, j, be, bv, nu: (layer, be[i], jj(i, j, nu), 0)),
            pl.BlockSpec((1, 1, 1, tf), lambda i, j, be, bv, nu: (layer, be[i], 0, jj(i, j, nu))),
            pl.BlockSpec((1, 1, 1, tf), lambda i, j, be, bv, nu: (layer, be[i], 0, nf + jj(i, j, nu))),
            pl.BlockSpec((1, 1, 1, d), lambda i, j, be, bv, nu: (layer, be[i], 0, 0)),
        ],
        out_specs=pl.BlockSpec((tm, d), row_blk),
        scratch_shapes=[pltpu.VMEM((tm, d), BF16)],
    )
    nl = w_gu.shape[0]
    return pl.pallas_call(
        _moe_body,
        grid_spec=grid_spec,
        out_shape=jax.ShapeDtypeStruct((n_slots, d), F32),
        compiler_params=_cparams(("arbitrary", "arbitrary")),
        name="moe",
    )(blk_e, blk_valid, n_used, xb, w_gu, w_gu, w_dn,
      b_gu.reshape(nl, n_exp, 1, two_f), b_gu.reshape(nl, n_exp, 1, two_f), b_dn.reshape(nl, n_exp, 1, d))


def _combine_body(dest_ref, gate_ref, x_ref, mod_ref, fg_ref, yb_ref, oc_ref, ol_ref, buf, sem,
                  *, final, n_ctx_tiles):
    rows = x_ref.shape[0]

    def row_copy(r, k, d):
        return pltpu.make_async_copy(yb_ref.at[pl.ds(d, 1)], buf.at[k, pl.ds(r, 1)], sem)

    def issue(r, carry):
        for k in range(TOP_K):
            row_copy(r, k, dest_ref[0, 0, r * TOP_K + k]).start()
        return carry

    def drain(r, carry):
        for k in range(TOP_K):
            row_copy(r, k, dest_ref[0, 0, r * TOP_K + k]).wait()
        return carry

    lax.fori_loop(0, rows, issue, 0)
    lax.fori_loop(0, rows, drain, 0)

    ff = gate_ref[:, 0:1] * buf[0]
    for k in range(1, TOP_K):
        ff = ff + gate_ref[:, k:k + 1] * buf[k]
    x_new = x_ref[...] + mod_ref[0, 5:6, :] * ff
    if final:
        x_new = _rms_lanes(x_new) * fg_ref[...]
    is_ctx = pl.program_id(0) < n_ctx_tiles

    @pl.when(is_ctx)
    def _():
        oc_ref[...] = x_new

    @pl.when(jnp.logical_not(is_ctx))
    def _():
        ol_ref[...] = x_new


def _combine(dest_tiles, gates, x, mod, final_g, yb, *, t_ctx, ds, final):
    t, d = x.shape
    n_ctx_tiles = t_ctx // TILE
    grp = functools.partial(_group_of_block, rows=TILE, t_ctx=t_ctx, ds=ds)
    return pl.pallas_call(
        functools.partial(_combine_body, final=final, n_ctx_tiles=n_ctx_tiles),
        grid=(t // TILE,),
        in_specs=[
            pl.BlockSpec((1, 1, TILE * TOP_K), lambda i: (i, 0, 0), memory_space=pltpu.SMEM),
            pl.BlockSpec((TILE, LANES), lambda i: (i, 0)),
            pl.BlockSpec((TILE, d), lambda i: (i, 0)),
            pl.BlockSpec((1, 6, d), lambda i: (grp(i), 0, 0)),
            pl.BlockSpec((1, d), lambda i: (0, 0)),
            pl.BlockSpec(memory_space=pl.ANY),
        ],
        out_specs=_pair_specs(TILE, d, n_ctx_tiles),
        out_shape=[jax.ShapeDtypeStruct((t_ctx, d), F32), jax.ShapeDtypeStruct((t - t_ctx, d), F32)],
        scratch_shapes=[pltpu.VMEM((TOP_K, TILE, d), F32), pltpu.SemaphoreType.DMA(())],
        compiler_params=_cparams(("arbitrary",)),
        name="combine",
    )(dest_tiles, gates, x, mod, final_g, yb)


def _slot_layout(idx, rank, counts, n_blocks):
    n_exp = counts.shape[0]
    padded = (counts + TM_MOE - 1) // TM_MOE * TM_MOE
    pend = jnp.cumsum(padded)
    pstart = pend - padded
    dest = pstart[idx] + rank
    bstart = jnp.arange(n_blocks, dtype=I32) * TM_MOE
    n_used = (pend[-1] // TM_MOE).astype(I32).reshape(1)
    owner_of = jnp.minimum(bstart, jnp.maximum(pend[-1] - TM_MOE, 0))
    blk_e = jnp.sum((owner_of[:, None] >= pend[None, :]).astype(I32), axis=1)
    blk_valid = jnp.clip(pstart[blk_e] + counts[blk_e] - bstart, 0, TM_MOE).astype(I32)
    return dest.astype(I32), blk_e.astype(I32), blk_valid, n_used


def kernel(x_prompt, x_sample, cache_k, cache_v, c, c_ctx, norm1_g, norm2_g, w_ada, b_ada, w_in, w_out,
           lam_q1, lam_k1, lam_q2, lam_k2, diff_ln_g, gmlp_norm_g, gmlp_ws, gmlp_bs, pool_w, pool_scale,
           router_w, router_b, exp_w_gu, exp_b_gu, exp_w_down, exp_b_down, final_g):
    nb_ctx, s_ctx, d = x_prompt.shape
    nb_lat, ds, _ = x_sample.shape
    depth = w_in.shape[0]
    n_exp = router_w.shape[-1]
    p_len = cache_k.shape[2]
    t_ctx, t_lat = nb_ctx * s_ctx, nb_lat * ds
    t = t_ctx + t_lat
    assert t_ctx % TILE == 0 and t_lat % TILE == 0 and (t * TOP_K) % TM_MOE == 0

    x_ctx, x_lat = x_prompt.reshape(t_ctx, d), x_sample.reshape(t_lat, d)
    cond_t = jnp.concatenate([c_ctx[None, :], c], axis=0).T
    mod = _ada(cond_t, w_ada, b_ada)
    mod = mod.reshape(depth, 1 + nb_lat, 6, d)

    rope_tabs = _rope_tables(ds)
    ck = cache_k.reshape(nb_lat, depth, p_len, A_HEADS * HEAD_W)
    cv = cache_v.reshape(nb_lat, depth, p_len, A_WIDTH)
    w_in_b = w_in.astype(BF16)
    w_out_b = w_out.astype(BF16)
    ws_b = gmlp_ws.astype(BF16)
    pw_b = pool_w.astype(BF16)
    rw_pad = jnp.pad(router_w, ((0, 0), (0, 0), (0, LANES - n_exp)))
    rw_hi = rw_pad.astype(BF16)
    rw_parts = jnp.stack([rw_hi, (rw_pad - rw_hi.astype(F32)).astype(BF16)], axis=1)
    rb_pad = jnp.pad(router_b, ((0, 0), (0, LANES - n_exp)))
    n_blocks = t * TOP_K // TM_MOE + n_exp
    n_slots = n_blocks * TM_MOE

    new_k, new_v = [], []
    for l in range(depth):
        lam_init = 0.8 - 0.6 * math.exp(-0.3 * l)
        lamp = jnp.stack([lam_q1[l], lam_k1[l], lam_q2[l], lam_k2[l]])
        lng = diff_ln_g[l][None, :]
        z = _inproj(x_ctx, x_lat, mod[l], norm1_g[l][None, :], w_in_b[l], ds=ds)
        oa_ctx, k_new, v_new = _attn_ctx(z, lamp, lng, nb=nb_ctx, s=s_ctx, lam_init=lam_init)
        new_k.append(k_new.reshape(nb_ctx, s_ctx, A_HEADS, HEAD_W))
        new_v.append(v_new.reshape(nb_ctx, s_ctx, A_HEADS, V_DIM))
        oa_lat = _attn_lat(z, ck, cv, rope_tabs, lamp, lng, layer=l, t_ctx=t_ctx, nb=nb_lat, ds=ds,
                           lam_init=lam_init)
        gb = jnp.repeat(gmlp_bs[l].T, B_CH, axis=1)
        x_mid, h_packed, logits = _mixpost(
            oa_ctx, oa_lat, z, x_ctx, x_lat, mod[l], norm2_g[l][None, :], w_out_b[l], gmlp_norm_g[l][None, :],
            ws_b[l], gb, pw_b[l], pool_scale[l][None, :], rw_parts[l], rb_pad[l][None, :], s_ctx=s_ctx, ds=ds)
        idx, gates, rank, counts = _route(logits, n_exp)
        dest, blk_e, blk_valid, n_used = _slot_layout(
            idx[:, :TOP_K], rank[:, :TOP_K], counts[0, :n_exp].astype(I32), n_blocks)
        dest_tiles = dest.reshape(t // TILE, 1, TILE * TOP_K)
        xb = _dispatch(dest_tiles, h_packed, n_slots)
        yb = _moe(blk_e, blk_valid, n_used, xb, exp_w_gu, exp_b_gu, exp_w_down, exp_b_down, layer=l)
        x_ctx, x_lat = _combine(dest_tiles, gates, x_mid, mod[l], final_g[None, :], yb, t_ctx=t_ctx, ds=ds,
                                final=(l == depth - 1))

    return (x_ctx.reshape(nb_ctx, s_ctx, d), x_lat.reshape(nb_lat, ds, d),
            jnp.stack(new_k, axis=1), jnp.stack(new_v, axis=1))
```

```python
import functools
import math

import numpy as np
import jax
import jax.numpy as jnp
from jax import lax
from jax.experimental import pallas as pl
from jax.experimental.pallas import tpu as pltpu

F32 = jnp.float32
BF16 = jnp.bfloat16
I32 = jnp.int32
U32 = jnp.uint32

GRID_W = 64
A_HEADS = 8
QK_DIM = 64
V_DIM = 2 * QK_DIM
HEAD_W = 2 * QK_DIM
Q_W = A_HEADS * HEAD_W
A_WIDTH = A_HEADS * V_DIM
B_GROUPS = 4
CHUNK = 128
B_CH = 128
B_WIDTH = B_GROUPS * B_CH
POOL_WINDOWS = (2, 4, 8, 16)
N_POOL = len(POOL_WINDOWS)
C_CH = 128
C_WIDTH = N_POOL * C_CH
ROPE_BASE = 10000.0
N_FREQ = QK_DIM // 4
TOP_K = 4
SWIGLU_ALPHA = 1.702
SWIGLU_LIMIT = 7.0
EPS = 1e-6
QK_SCALE = QK_DIM ** -0.5

LANES = 128
TILE = 256
HALO = 16
TM_PROJ_CHOICES = (1024, 512, 256)
LOG2E = 1.4426950408889634
TM_MOE = 1024
SUB_MOE = 256
TF_MOE = 256
TQ_ATT = 512
VMEM_LIMIT = 56 * 1024 * 1024


def _cparams(sem, vmem=VMEM_LIMIT):
    return pltpu.CompilerParams(dimension_semantics=sem, vmem_limit_bytes=vmem)


def _norm_mod(x, g, shift, scale):
    ms = jnp.mean(x * x, axis=-1, keepdims=True)
    return x * lax.rsqrt(ms + EPS) * g * (1.0 + scale) + shift


def _rms_lanes(x):
    return x * lax.rsqrt(jnp.mean(x * x, axis=-1, keepdims=True) + EPS)


def _dot(a, b):
    return jnp.dot(a, b, preferred_element_type=F32)


def _dot_nt(a, b):
    return lax.dot_general(a, b, (((1,), (1,)), ((), ())), preferred_element_type=F32)


def _ada_body(ct_ref, w_ref, b_ref, o_ref, sb, *, kc):
    d, ng = ct_ref.shape
    tn = w_ref.shape[-1]

    @pl.when(jnp.logical_and(pl.program_id(0) == 0, pl.program_id(1) == 0))
    def _():
        def fill(c, carry):
            k0 = pl.multiple_of(c * kc, kc)
            s = jax.nn.silu(ct_ref[pl.ds(k0, kc), :])
            for gi in range(ng):
                sb[gi, pl.ds(k0, kc), :] = jnp.broadcast_to(s[:, gi:gi + 1], (kc, LANES))
            return carry

        lax.fori_loop(0, d // kc, fill, 0)

    def body(c, accs):
        k0 = pl.multiple_of(c * kc, kc)
        w = w_ref[0, pl.ds(k0, kc), :]
        out = []
        for gi, acc in enumerate(accs):
            s = sb[gi, pl.ds(k0, kc), :]
            parts = [jnp.sum((w[:, t * LANES:(t + 1) * LANES] * s).reshape(kc // 8, 8, LANES), axis=0)
                     for t in range(tn // LANES)]
            out.append(acc + jnp.concatenate(parts, axis=1))
        return tuple(out)

    accs = lax.fori_loop(0, d // kc, body, tuple(jnp.zeros((8, tn), F32) for _ in range(ng)))
    for gi in range(ng):
        o_ref[0, gi:gi + 1, :] = jnp.sum(accs[gi], axis=0, keepdims=True) + b_ref[0]


def _ada(cond_t, w_ada, b_ada):
    d, ng = cond_t.shape
    nl, _, n = w_ada.shape
    tn = 1024
    return pl.pallas_call(
        functools.partial(_ada_body, kc=32),
        grid=(nl, n // tn),
        in_specs=[
            pl.BlockSpec((d, ng), lambda l, j: (0, 0)),
            pl.BlockSpec((1, d, tn), lambda l, j: (l, 0, j)),
            pl.BlockSpec((1, 1, tn), lambda l, j: (l, 0, j)),
        ],
        out_specs=pl.BlockSpec((1, ng, tn), lambda l, j: (l, 0, j)),
        out_shape=jax.ShapeDtypeStruct((nl, ng, n), F32),
        scratch_shapes=[pltpu.VMEM((ng, d, LANES), F32)],
        compiler_params=_cparams(("arbitrary", "arbitrary")),
        name="ada",
    )(cond_t, w_ada, b_ada.reshape(nl, 1, n))


def _inproj_body(xc_ref, xl_ref, mod_ref, g_ref, w_ref, z_ref, h_scr, *, rc, n_ctx_blocks):
    @pl.when(pl.program_id(1) == 0)
    def _():
        shift = mod_ref[0, 0:1, :]
        scale = mod_ref[0, 1:2, :]
        g = g_ref[...]

        def fill(x_ref):
            def body(c, carry):
                r0 = pl.multiple_of(c * rc, rc)
                h = _norm_mod(x_ref[pl.ds(r0, rc), :], g, shift, scale)
                h_scr[pl.ds(r0, rc), :] = h.astype(BF16)
                return carry

            lax.fori_loop(0, x_ref.shape[0] // rc, body, 0)

        is_ctx = pl.program_id(0) < n_ctx_blocks
        pl.when(is_ctx)(lambda: fill(xc_ref))
        pl.when(jnp.logical_not(is_ctx))(lambda: fill(xl_ref))

    z_ref[...] = _dot(h_scr[...], w_ref[...])


def _group_of_block(i, rows, t_ctx, ds):
    r = i * rows
    return jnp.where(r < t_ctx, 0, 1 + (r - t_ctx) // ds)


def _pair_specs(rows, width, n_ctx_blocks):
    return [
        pl.BlockSpec((rows, width), lambda i, *_: (jnp.minimum(i, n_ctx_blocks - 1), 0)),
        pl.BlockSpec((rows, width), lambda i, *_: (jnp.maximum(i - n_ctx_blocks, 0), 0)),
    ]


def _inproj(x_ctx, x_lat, mod, g1, w_in, *, ds):
    t_ctx, d = x_ctx.shape
    t = t_ctx + x_lat.shape[0]
    n = w_in.shape[1]
    tm = next(m for m in TM_PROJ_CHOICES if t_ctx % m == 0 and ds % m == 0)
    tn = 768
    assert n % tn == 0
    grp = functools.partial(_group_of_block, rows=tm, t_ctx=t_ctx, ds=ds)
    return pl.pallas_call(
        functools.partial(_inproj_body, rc=64, n_ctx_blocks=t_ctx // tm),
        grid=(t // tm, n // tn),
        in_specs=_pair_specs(tm, d, t_ctx // tm) + [
            pl.BlockSpec((1, 6, d), lambda i, j: (grp(i), 0, 0)),
            pl.BlockSpec((1, d), lambda i, j: (0, 0)),
            pl.BlockSpec((d, tn), lambda i, j: (0, j)),
        ],
        out_specs=pl.BlockSpec((tm, tn), lambda i, j: (i, j)),
        out_shape=jax.ShapeDtypeStruct((t, n), F32),
        scratch_shapes=[pltpu.VMEM((tm, d), BF16)],
        compiler_params=_cparams(("arbitrary", "arbitrary")),
        name="inproj",
    )(x_ctx, x_lat, mod, g1, w_in)


def _lam_value(lamp_ref, lam_init):
    lp = lamp_ref[...]
    a = jnp.sum(lp[0:1] * lp[1:2], keepdims=True)
    b = jnp.sum(lp[2:3] * lp[3:4], keepdims=True)
    return jnp.exp(a) - jnp.exp(b) + lam_init


def _split_halves(q):
    first = lax.broadcasted_iota(I32, (1, HEAD_W), 1) < QK_DIM
    return jnp.where(first, q, 0.0).astype(BF16), jnp.where(first, 0.0, q).astype(BF16)


def _attn_ctx_body(lamp_ref, lng_ref, q_ref, k_ref, v_ref, o_ref, ko_ref, vo_ref, *, lam_init):
    lam = _lam_value(lamp_ref, lam_init)
    g = lng_ref[...] * (1.0 - lam_init)
    ko_ref[...] = k_ref[...]
    vo_ref[...] = v_ref[...]
    for h in range(A_HEADS):
        hs = slice(h * HEAD_W, (h + 1) * HEAD_W)
        k = k_ref[:, hs].astype(BF16)
        v = v_ref[:, hs].astype(BF16)
        ps = []
        for qh in _split_halves(q_ref[:, hs] * QK_SCALE):
            s = _dot_nt(qh, k)
            e = jnp.exp(s - jnp.max(s, axis=-1, keepdims=True))
            ps.append(e * (1.0 / jnp.sum(e, axis=-1, keepdims=True)))
        o = _dot((ps[0] - lam * ps[1]).astype(BF16), v)
        o_ref[:, hs] = (_rms_lanes(o) * g).astype(o_ref.dtype)


def _attn_ctx(z, lamp, lng, *, nb, s, lam_init):
    return pl.pallas_call(
        functools.partial(_attn_ctx_body, lam_init=lam_init),
        grid=(nb,),
        in_specs=[
            pl.BlockSpec(lamp.shape, lambda b: (0, 0)),
            pl.BlockSpec((1, V_DIM), lambda b: (0, 0)),
            pl.BlockSpec((s, Q_W), lambda b: (b, 0)),
            pl.BlockSpec((s, Q_W), lambda b: (b, 1)),
            pl.BlockSpec((s, A_WIDTH), lambda b: (b, 2)),
        ],
        out_specs=[pl.BlockSpec((s, A_WIDTH), lambda b: (b, 0))] * 3,
        out_shape=[
            jax.ShapeDtypeStruct((nb * s, A_WIDTH), BF16),
            jax.ShapeDtypeStruct((nb * s, Q_W), F32),
            jax.ShapeDtypeStruct((nb * s, A_WIDTH), F32),
        ],
        compiler_params=_cparams(("parallel",)),
        name="attn_ctx",
    )(lamp, lng, z, z, z)


def _rope(x, c, sa, sb):
    return x * c + pltpu.roll(x, HEAD_W - N_FREQ, 1) * sa + pltpu.roll(x, N_FREQ, 1) * sb


def _attn_lat_body(lamp_ref, lng_ref, q_ref, k_ref, v_ref, ck_ref, cv_ref,
                   cq_ref, saq_ref, sbq_ref, ckk_ref, sak_ref, sbk_ref,
                   o_ref, kall, vall, s_scr, e_scr, *, lam_init, kc, rc):
    ds = k_ref.shape[0]
    p = ck_ref.shape[2]
    tq = q_ref.shape[0]
    nch = (ds + p) // kc

    @pl.when(pl.program_id(2) == 0)
    def _():
        def ones_col(rows):
            return jnp.where(lax.broadcasted_iota(I32, (rows, LANES), 1) == 0, 1.0, 0.0).astype(BF16)

        def body(c, carry):
            r0 = pl.multiple_of(c * rc, rc)
            rs = pl.ds(r0, rc)
            kall[rs, :] = _rope(k_ref[rs, :], ckk_ref[rs, :], sak_ref[rs, :], sbk_ref[rs, :]).astype(BF16)
            vall[rs, 0:V_DIM] = v_ref[rs, :].astype(BF16)
            vall[rs, V_DIM:] = ones_col(rc)
            return carry

        lax.fori_loop(0, ds // rc, body, 0)
        kall[ds:ds + p, :] = ck_ref[0, 0].astype(BF16)
        vall[ds:ds + p, 0:V_DIM] = cv_ref[0, 0].astype(BF16)
        vall[ds:ds + p, V_DIM:] = ones_col(p)

    lam = _lam_value(lamp_ref, lam_init)
    q = _rope(q_ref[...], cq_ref[...], saq_ref[...], sbq_ref[...]) * (QK_SCALE * LOG2E)
    halves = _split_halves(q)
    chunks = [slice(c * kc, (c + 1) * kc) for c in range(nch)]

    def qk(n, ch, mrun):
        s = _dot_nt(halves[n], kall[ch, :])
        s_scr[n, :, ch] = s
        for t in range(kc // LANES):
            mrun = jnp.maximum(mrun, s[:, t * LANES:(t + 1) * LANES])
        return mrun

    def ex(n, ch, m):
        e_scr[n, :, ch] = jnp.exp2(s_scr[n, :, ch] - m).astype(BF16)

    neg = jnp.full((tq, LANES), -jnp.inf, F32)
    mrun = neg
    for ch in chunks:
        mrun = qk(0, ch, mrun)
    m0 = jnp.max(mrun, axis=-1, keepdims=True)
    mrun = neg
    for ch in chunks:
        mrun = qk(1, ch, mrun)
        ex(0, ch, m0)
    m1 = jnp.max(mrun, axis=-1, keepdims=True)
    oe0 = jnp.zeros((tq, V_DIM + LANES), F32)
    for ch in chunks:
        ex(1, ch, m1)
        oe0 = oe0 + _dot(e_scr[0, :, ch], vall[ch, :])
    oe1 = _dot(e_scr[1], vall[...])
    outs = [oe[:, 0:V_DIM] * (1.0 / oe[:, V_DIM:V_DIM + 1]) for oe in (oe0, oe1)]
    o = outs[0] - lam * outs[1]
    o_ref[...] = (_rms_lanes(o) * (lng_ref[...] * (1.0 - lam_init))).astype(o_ref.dtype)


def _attn_lat(z, cache_k, cache_v, rope_tabs, lamp, lng, *, layer, t_ctx, nb, ds, lam_init):
    p = cache_k.shape[2]
    tq = min(TQ_ATT, ds)
    sk = ds + p
    kc = 512 if sk % 512 == 0 else LANES
    assert t_ctx % ds == 0 and ds % tq == 0 and sk % kc == 0
    seq0, q0 = t_ctx // ds, t_ctx // tq
    nq = ds // tq
    qh, kh, vh = 0, Q_W // HEAD_W, 2 * Q_W // HEAD_W
    tab_q = pl.BlockSpec((tq, HEAD_W), lambda b, h, qi: (qi, 0))
    tab_k = pl.BlockSpec((ds, HEAD_W), lambda b, h, qi: (0, 0))
    cos, sa, sb = rope_tabs
    return pl.pallas_call(
        functools.partial(_attn_lat_body, lam_init=lam_init, kc=kc, rc=256),
        grid=(nb, A_HEADS, nq),
        in_specs=[
            pl.BlockSpec(lamp.shape, lambda b, h, qi: (0, 0)),
            pl.BlockSpec((1, V_DIM), lambda b, h, qi: (0, 0)),
            pl.BlockSpec((tq, HEAD_W), lambda b, h, qi: (q0 + b * nq + qi, qh + h)),
            pl.BlockSpec((ds, HEAD_W), lambda b, h, qi: (seq0 + b, kh + h)),
            pl.BlockSpec((ds, HEAD_W), lambda b, h, qi: (seq0 + b, vh + h)),
            pl.BlockSpec((1, 1, p, HEAD_W), lambda b, h, qi: (b, layer, 0, h)),
            pl.BlockSpec((1, 1, p, V_DIM), lambda b, h, qi: (b, layer, 0, h)),
            tab_q, tab_q, tab_q, tab_k, tab_k, tab_k,
        ],
        out_specs=pl.BlockSpec((tq, V_DIM), lambda b, h, qi: (b * nq + qi, h)),
        out_shape=jax.ShapeDtypeStruct((nb * ds, A_WIDTH), BF16),
        scratch_shapes=[
            pltpu.VMEM((sk, HEAD_W), BF16),
            pltpu.VMEM((sk, V_DIM + LANES), BF16),
            pltpu.VMEM((2, tq, sk), F32),
            pltpu.VMEM((2, tq, sk), BF16),
        ],
        compiler_params=_cparams(("parallel", "parallel", "arbitrary")),
        name="attn_lat",
    )(lamp, lng, z, z, z, cache_k, cache_v, cos, sa, sb, cos, sa, sb)


def _rope_tables(n_tok):
    n_rows = n_tok // GRID_W
    row = jnp.repeat(jnp.arange(n_rows), GRID_W).astype(F32)
    col = jnp.tile(jnp.arange(GRID_W), n_rows).astype(F32)
    inv = 1.0 / (ROPE_BASE ** (jnp.arange(N_FREQ, dtype=F32) / N_FREQ))
    ang = jnp.stack([row[:, None] * inv, col[:, None] * inv], axis=1)
    cos, sin = jnp.cos(ang), jnp.sin(ang)
    zero = jnp.zeros_like(sin)

    def lanes(first, second):
        per_axis = jnp.concatenate([first, second], axis=-1)
        return jnp.tile(per_axis.reshape(n_tok, 2 * 2 * N_FREQ), (1, 2))

    return lanes(cos, cos), lanes(-sin, zero), lanes(zero, sin)


def _pack_pair(hi, lo):
    hb = lax.bitcast_convert_type(hi.astype(BF16).astype(F32), U32)
    lb = lax.bitcast_convert_type(lo.astype(BF16).astype(F32), U32)
    return hb | (lb >> 16)


def _unpack_pair(u):
    hi = lax.bitcast_convert_type(u & jnp.uint32(0xFFFF0000), F32)
    lo = lax.bitcast_convert_type(u << 16, F32)
    return hi.astype(BF16), lo.astype(BF16)


def _hi_lo(x):
    hi = x.astype(BF16)
    return hi, (x - hi.astype(F32)).astype(BF16)


def _mixpost_body(oac_ref, oal_ref, zb_ref, zc_ref, zp_ref, zn_ref, xc_ref, xl_ref, mod_ref, g2_ref,
                  wout_ref, gng_ref, ws_ref, gb_ref, pw_ref, ps_ref, band_ref, bandp_ref, bandn_ref,
                  rw_ref, rb_ref, xo_ref, h_ref, lg_ref, cat_scr, *, n_ctx_tiles, tps_ctx, tps_lat):
    i = pl.program_id(0)
    is_ctx = i < n_ctx_tiles
    pos = jnp.where(is_ctx, i % tps_ctx, (i - n_ctx_tiles) % tps_lat)
    first = pos == 0
    last = pos == jnp.where(is_ctx, tps_ctx, tps_lat) - 1

    cat_scr[:, 0:A_WIDTH] = jnp.where(is_ctx, oac_ref[...], oal_ref[...])

    zb = jax.nn.gelu(zb_ref[...])
    u, v = zb[:, :B_WIDTH], zb[:, B_WIDTH:]
    for g in range(B_GROUPS):
        gs = slice(g * B_CH, (g + 1) * B_CH)
        vn = (_rms_lanes(v[:, gs]) * gng_ref[:, gs]).astype(BF16)
        for n in range(TILE // CHUNK):
            rs = slice(n * CHUNK, (n + 1) * CHUNK)
            mixed = _dot(ws_ref[g], vn[rs]) + gb_ref[:, gs]
            cat_scr[rs, A_WIDTH + g * B_CH:A_WIDTH + (g + 1) * B_CH] = (u[rs, gs] * mixed).astype(BF16)

    xc = zc_ref[...]
    xc_parts = _hi_lo(xc)
    xp_parts = _hi_lo(zp_ref[...] * jnp.where(first, 0.0, 1.0))
    xn_parts = _hi_lo(zn_ref[...] * jnp.where(last, 0.0, 1.0))
    r = lax.broadcasted_iota(I32, (TILE, 1), 0)
    for g, w in enumerate(POOL_WINDOWS):
        gs = slice(g * C_CH, (g + 1) * C_CH)
        half = w // 2
        acc = jnp.zeros((TILE, C_CH), F32)
        for part in xc_parts:
            acc = acc + _dot(band_ref[g], part[:, gs])
        for part in xp_parts:
            acc = acc + _dot(bandp_ref[g], part[:, gs])
        for part in xn_parts:
            acc = acc + _dot(bandn_ref[g], part[:, gs])
        left = jnp.where(first, jnp.minimum(half, r), half)
        right = jnp.where(last, jnp.minimum(half - 1, TILE - 1 - r), half - 1)
        cnt = (left + right + 1).astype(F32)
        pooled = acc / cnt - xc[:, gs]
        y = _dot(pooled.astype(BF16), pw_ref[g]) * ps_ref[:, gs]
        c0 = A_WIDTH + B_WIDTH + g * C_CH
        cat_scr[:, c0:c0 + C_CH] = y.astype(BF16)

    mix = _dot(cat_scr[...], wout_ref[...])
    x_new = jnp.where(is_ctx, xc_ref[...], xl_ref[...]) + mod_ref[0, 2:3, :] * mix
    xo_ref[...] = x_new
    h2 = _norm_mod(x_new, g2_ref[...], mod_ref[0, 3:4, :], mod_ref[0, 4:5, :])
    half_d = h2.shape[1] // 2
    h_ref[...] = _pack_pair(h2[:, :half_d], h2[:, half_d:])
    h_hi, h_lo = _hi_lo(h2)
    lg_ref[...] = _dot(h_hi, rw_ref[0]) + (_dot(h_lo, rw_ref[0]) + _dot(h_hi, rw_ref[1])) + rb_ref[...]


def _band_matrices():
    r = np.arange(TILE)[:, None]
    cur, prev, nxt = [], [], []
    for w in POOL_WINDOWS:
        half = w // 2
        c = np.arange(TILE)[None, :]
        cur.append((c >= r - half) & (c <= r + half - 1))
        ch = np.arange(HALO)[None, :]
        prev.append(ch - HALO >= r - half)
        nxt.append(TILE + ch <= r + half - 1)
    to = lambda m: jnp.asarray(np.stack(m).astype(np.float32), dtype=BF16)
    return to(cur), to(prev), to(nxt)


def _mixpost(oa_ctx, oa_lat, z, x_ctx, x_lat, mod, g2, w_out, gng, ws, gb, pw, ps, rw, rb, *, s_ctx, ds):
    t_ctx, d = x_ctx.shape
    t = t_ctx + x_lat.shape[0]
    nt = t // TILE
    n_ctx_tiles = t_ctx // TILE
    tps_ctx, tps_lat = s_ctx // TILE, ds // TILE
    assert s_ctx % TILE == 0 and ds % TILE == 0
    grp = functools.partial(_group_of_block, rows=TILE, t_ctx=t_ctx, ds=ds)
    band, bandp, bandn = _band_matrices()
    zb_blk = (2 * Q_W + A_WIDTH) // (2 * B_WIDTH)
    zc_blk = (2 * Q_W + A_WIDTH + 2 * B_WIDTH) // C_WIDTH
    hpt = TILE // HALO
    const2 = lambda i: (0, 0)
    const3 = lambda i: (0, 0, 0)
    return pl.pallas_call(
        functools.partial(_mixpost_body, n_ctx_tiles=n_ctx_tiles, tps_ctx=tps_ctx, tps_lat=tps_lat),
        grid=(nt,),
        in_specs=_pair_specs(TILE, A_WIDTH, n_ctx_tiles) + [
            pl.BlockSpec((TILE, 2 * B_WIDTH), lambda i: (i, zb_blk)),
            pl.BlockSpec((TILE, C_WIDTH), lambda i: (i, zc_blk)),
            pl.BlockSpec((HALO, C_WIDTH), lambda i: (jnp.maximum(i * hpt - 1, 0), zc_blk)),
            pl.BlockSpec((HALO, C_WIDTH), lambda i: (jnp.minimum((i + 1) * hpt, nt * hpt - 1), zc_blk)),
        ] + _pair_specs(TILE, d, n_ctx_tiles) + [
            pl.BlockSpec((1, 6, d), lambda i: (grp(i), 0, 0)),
            pl.BlockSpec((1, d), const2),
            pl.BlockSpec(w_out.shape, const2),
            pl.BlockSpec((1, B_WIDTH), const2),
            pl.BlockSpec(ws.shape, const3),
            pl.BlockSpec(gb.shape, const2),
            pl.BlockSpec(pw.shape, const3),
            pl.BlockSpec((1, C_WIDTH), const2),
            pl.BlockSpec(band.shape, const3),
            pl.BlockSpec(bandp.shape, const3),
            pl.BlockSpec(bandn.shape, const3),
            pl.BlockSpec(rw.shape, const3),
            pl.BlockSpec(rb.shape, const2),
        ],
        out_specs=[
            pl.BlockSpec((TILE, d), lambda i: (i, 0)),
            pl.BlockSpec((TILE, d // 2), lambda i: (i, 0)),
            pl.BlockSpec((TILE, LANES), lambda i: (i, 0)),
        ],
        out_shape=[
            jax.ShapeDtypeStruct((t, d), F32),
            jax.ShapeDtypeStruct((t, d // 2), U32),
            jax.ShapeDtypeStruct((t, LANES), F32),
        ],
        scratch_shapes=[pltpu.VMEM((TILE, d), BF16)],
        compiler_params=_cparams(("arbitrary",)),
        name="mixpost",
    )(oa_ctx, oa_lat, z, z, z, z, x_ctx, x_lat, mod, g2, w_out, gng, ws, gb, pw, ps, band, bandp, bandn, rw, rb)


def _route_body(lg_ref, tri_ref, idx_ref, gate_ref, rank_ref, cnt_ref, carry, *, n_exp):
    @pl.when(pl.program_id(0) == 0)
    def _():
        carry[...] = jnp.zeros_like(carry)

    lane = lax.broadcasted_iota(I32, lg_ref.shape, 1).astype(F32)
    l = jnp.where(lane < n_exp, lg_ref[...], -jnp.inf)
    vals, idxs, sels = [], [], []
    for _ in range(TOP_K):
        m = jnp.max(l, axis=-1, keepdims=True)
        idx = jnp.min(jnp.where(l == m, lane, float(LANES)), axis=-1, keepdims=True)
        sel = lane == idx
        vals.append(m)
        idxs.append(idx)
        sels.append(sel)
        l = jnp.where(sel, -jnp.inf, l)
    chosen = functools.reduce(jnp.logical_or, sels)
    chosen_f = jnp.where(chosen, 1.0, 0.0)
    prefix = carry[...] + _dot(tri_ref[...], chosen_f.astype(BF16))
    es = [jnp.exp(v - vals[0]) for v in vals]
    tot = functools.reduce(jnp.add, es)
    idx_o = jnp.zeros(lg_ref.shape, F32)
    gate_o = jnp.zeros(lg_ref.shape, F32)
    rank_o = jnp.zeros(lg_ref.shape, F32)
    for k in range(TOP_K):
        rank_k = jnp.sum(jnp.where(sels[k], prefix, 0.0), axis=-1, keepdims=True)
        idx_o = jnp.where(lane == k, idxs[k], idx_o)
        gate_o = jnp.where(lane == k, es[k] / tot, gate_o)
        rank_o = jnp.where(lane == k, rank_k, rank_o)
    idx_ref[...] = idx_o.astype(I32)
    gate_ref[...] = gate_o
    rank_ref[...] = rank_o.astype(I32)
    carry[...] = carry[...] + jnp.sum(chosen_f, axis=0, keepdims=True)
    cnt_ref[...] = carry[...]


def _route(logits, n_exp):
    t = logits.shape[0]
    tri = jnp.asarray(np.tril(np.ones((TILE, TILE), np.float32), -1), dtype=BF16)
    blk = pl.BlockSpec((TILE, LANES), lambda i: (i, 0))
    return pl.pallas_call(
        functools.partial(_route_body, n_exp=n_exp),
        grid=(t // TILE,),
        in_specs=[blk, pl.BlockSpec((TILE, TILE), lambda i: (0, 0))],
        out_specs=[blk, blk, blk, pl.BlockSpec((1, LANES), lambda i: (0, 0))],
        out_shape=[
            jax.ShapeDtypeStruct((t, LANES), I32),
            jax.ShapeDtypeStruct((t, LANES), F32),
            jax.ShapeDtypeStruct((t, LANES), I32),
            jax.ShapeDtypeStruct((1, LANES), F32),
        ],
        scratch_shapes=[pltpu.VMEM((1, LANES), F32)],
        compiler_params=_cparams(("arbitrary",)),
        name="route",
    )(logits, tri)


DMA_UNROLL = 8


def _moe_body(be_ref, bv_ref, nu_ref, tokc_ref, tokn_ref, h_ref, wg_ref, wu_ref, wd_ref, bg_ref, bu_ref,
              bd_ref, y_ref, xg, x_scr, sems, *, nf):
    del be_ref
    i, j = pl.program_id(0), pl.program_id(1)
    valid = bv_ref[i]
    tm, half_d = xg.shape[1], xg.shape[2]
    rows_per_step = tm // nf
    buf = i % 2

    def row_copy(tok, b, r):
        return pltpu.make_async_copy(h_ref.at[pl.ds(tok, 1)], xg.at[b, pl.ds(r, 1)], sems.at[b])

    def for_rows(b, tok_ref, fn):
        def body(g, carry):
            r0 = pl.multiple_of(g * DMA_UNROLL, DMA_UNROLL)
            for u in range(DMA_UNROLL):
                fn(row_copy(tok_ref[0, 0, r0 + u], b, r0 + u))
            return carry

        lax.fori_loop(0, tm // DMA_UNROLL, body, 0)

    @pl.when(valid > 0)
    def _():
        @pl.when(j == 0)
        def _():
            @pl.when(i == 0)
            def _():
                for_rows(0, tokc_ref, lambda cp: cp.start())

            for_rows(buf, tokc_ref, lambda cp: cp.wait())

            def unpack(s, carry):
                rs = pl.ds(pl.multiple_of(s * SUB_MOE, SUB_MOE), SUB_MOE)
                hi, lo = _unpack_pair(xg[buf, rs, :])
                x_scr[rs, 0:half_d] = hi
                x_scr[rs, half_d:] = lo
                y_ref[rs, :] = jnp.broadcast_to(bd_ref[0, 0], (SUB_MOE, y_ref.shape[1]))
                return carry

            lax.fori_loop(0, tm // SUB_MOE, unpack, 0)

        def ffn(rows):
            base = j * rows_per_step
            for u in range(rows_per_step):
                row_copy(tokn_ref[0, 0, base + u], 1 - buf, base + u).start()
            x = x_scr[0:rows, :]
            g = _dot(x, wg_ref[0, 0].astype(BF16)) + bg_ref[0, 0]
            u = _dot(x, wu_ref[0, 0].astype(BF16)) + bu_ref[0, 0]
            g = jnp.minimum(g, SWIGLU_LIMIT)
            u = jnp.clip(u, -SWIGLU_LIMIT, SWIGLU_LIMIT)
            a = g * jax.nn.sigmoid(SWIGLU_ALPHA * g) * (u + 1.0)
            y_ref[0:rows, :] += _dot(a.astype(BF16), wd_ref[0, 0].astype(BF16))

        n_sub = (valid + SUB_MOE - 1) // SUB_MOE
        for k in range(1, tm // SUB_MOE + 1):
            pl.when(n_sub == k)(functools.partial(ffn, k * SUB_MOE))

        @pl.when(jnp.logical_and(i == nu_ref[0] - 1, j == nf - 1))
        def _():
            for_rows(1 - buf, tokn_ref, lambda cp: cp.wait())


def _moe(blk_e, blk_valid, n_used, slot_tok, h_packed, w_gu, b_gu, w_dn, b_dn, *, layer):
    half_d = h_packed.shape[1]
    d = 2 * half_d
    n_exp, _, two_f = w_gu.shape[1:]
    f = two_f // 2
    tm, tf = TM_MOE, TF_MOE
    nf = f // tf
    nb = slot_tok.shape[0] // tm
    n_slots = nb * tm
    assert tm % (nf * DMA_UNROLL) == 0

    def live(i, nu):
        return i < nu[0]

    def row_blk(i, j, be, bv, nu):
        return (jnp.minimum(i, nu[0] - 1), 0)

    def jj(i, j, nu):
        return jnp.where(live(i, nu), j, nf - 1)

    grid_spec = pltpu.PrefetchScalarGridSpec(
        num_scalar_prefetch=3,
        grid=(nb, nf),
        in_specs=[
            pl.BlockSpec((1, 1, tm), lambda i, j, be, bv, nu: (jnp.minimum(i, nu[0] - 1), 0, 0),
                         memory_space=pltpu.SMEM),
            pl.BlockSpec((1, 1, tm), lambda i, j, be, bv, nu: (jnp.minimum(i + 1, nu[0] - 1), 0, 0),
                         memory_space=pltpu.SMEM),
            pl.BlockSpec(memory_space=pl.ANY),
            pl.BlockSpec((1, 1, d, tf), lambda i, j, be, bv, nu: (layer, be[i], 0, jj(i, j, nu))),
            pl.BlockSpec((1, 1, d, tf), lambda i, j, be, bv, nu: (layer, be[i], 0, nf + jj(i, j, nu))),
            pl.BlockSpec((1, 1, tf, d), lambda i, j, be, bv, nu: (layer, be[i], jj(i, j, nu), 0)),
            pl.BlockSpec((1, 1, 1, tf), lambda i, j, be, bv, nu: (layer, be[i], 0, jj(i, j, nu))),
            pl.BlockSpec((1, 1, 1, tf), lambda i, j, be, bv, nu: (layer, be[i], 0, nf + jj(i, j, nu))),
            pl.BlockSpec((1, 1, 1, d), lambda i, j, be, bv, nu: (layer, be[i], 0, 0)),
        ],
        out_specs=pl.BlockSpec((tm, d), row_blk),
        scratch_shapes=[
            pltpu.VMEM((2, tm, half_d), U32),
            pltpu.VMEM((tm, d), BF16),
            pltpu.SemaphoreType.DMA((2,)),
        ],
    )
    nl = w_gu.shape[0]
    tok_blocks = slot_tok.reshape(nb, 1, tm)
    return pl.pallas_call(
        functools.partial(_moe_body, nf=nf),
        grid_spec=grid_spec,
        out_shape=jax.ShapeDtypeStruct((n_slots, d), F32),
        compiler_params=_cparams(("arbitrary", "arbitrary")),
        name="moe",
    )(blk_e, blk_valid, n_used, tok_blocks, tok_blocks, h_packed, w_gu, w_gu, w_dn,
      b_gu.reshape(nl, n_exp, 1, two_f), b_gu.reshape(nl, n_exp, 1, two_f), b_dn.reshape(nl, n_exp, 1, d))


def _combine_body(dest_ref, gate_ref, x_ref, mod_ref, fg_ref, yb_ref, oc_ref, ol_ref, buf, sem,
                  *, final, n_ctx_tiles):
    rows = x_ref.shape[0]

    def row_copy(r, k, d):
        return pltpu.make_async_copy(yb_ref.at[pl.ds(d, 1)], buf.at[k, pl.ds(r, 1)], sem)

    def for_rows(fn):
        def body(g, carry):
            r0 = pl.multiple_of(g * DMA_UNROLL, DMA_UNROLL)
            for u in range(DMA_UNROLL):
                for k in range(TOP_K):
                    fn(row_copy(r0 + u, k, dest_ref[0, 0, (r0 + u) * TOP_K + k]))
            return carry

        lax.fori_loop(0, rows // DMA_UNROLL, body, 0)

    for_rows(lambda cp: cp.start())
    for_rows(lambda cp: cp.wait())

    ff = gate_ref[:, 0:1] * buf[0]
    for k in range(1, TOP_K):
        ff = ff + gate_ref[:, k:k + 1] * buf[k]
    x_new = x_ref[...] + mod_ref[0, 5:6, :] * ff
    if final:
        x_new = _rms_lanes(x_new) * fg_ref[...]
    is_ctx = pl.program_id(0) < n_ctx_tiles

    @pl.when(is_ctx)
    def _():
        oc_ref[...] = x_new

    @pl.when(jnp.logical_not(is_ctx))
    def _():
        ol_ref[...] = x_new


def _combine(dest_tiles, gates, x, mod, final_g, yb, *, t_ctx, ds, final):
    t, d = x.shape
    n_ctx_tiles = t_ctx // TILE
    grp = functools.partial(_group_of_block, rows=TILE, t_ctx=t_ctx, ds=ds)
    return pl.pallas_call(
        functools.partial(_combine_body, final=final, n_ctx_tiles=n_ctx_tiles),
        grid=(t // TILE,),
        in_specs=[
            pl.BlockSpec((1, 1, TILE * TOP_K), lambda i: (i, 0, 0), memory_space=pltpu.SMEM),
            pl.BlockSpec((TILE, LANES), lambda i: (i, 0)),
            pl.BlockSpec((TILE, d), lambda i: (i, 0)),
            pl.BlockSpec((1, 6, d), lambda i: (grp(i), 0, 0)),
            pl.BlockSpec((1, d), lambda i: (0, 0)),
            pl.BlockSpec(memory_space=pl.ANY),
        ],
        out_specs=_pair_specs(TILE, d, n_ctx_tiles),
        out_shape=[jax.ShapeDtypeStruct((t_ctx, d), F32), jax.ShapeDtypeStruct((t - t_ctx, d), F32)],
        scratch_shapes=[pltpu.VMEM((TOP_K, TILE, d), F32), pltpu.SemaphoreType.DMA(())],
        compiler_params=_cparams(("arbitrary",)),
        name="combine",
    )(dest_tiles, gates, x, mod, final_g, yb)


def _slot_layout(idx, rank, counts, n_blocks):
    n_exp = counts.shape[0]
    padded = (counts + TM_MOE - 1) // TM_MOE * TM_MOE
    pend = jnp.cumsum(padded)
    pstart = pend - padded
    dest = pstart[idx] + rank
    bstart = jnp.arange(n_blocks, dtype=I32) * TM_MOE
    n_used = (pend[-1] // TM_MOE).astype(I32).reshape(1)
    owner_of = jnp.minimum(bstart, jnp.maximum(pend[-1] - TM_MOE, 0))
    blk_e = jnp.minimum(jnp.searchsorted(pend, owner_of, side='right'), n_exp - 1).astype(I32)
    blk_valid = jnp.clip(pstart[blk_e] + counts[blk_e] - bstart, 0, TM_MOE).astype(I32)
    n_tok = idx.shape[0]
    slot_tok = jnp.zeros((n_blocks * TM_MOE,), I32).at[dest.reshape(-1)].set(
        jnp.arange(n_tok * TOP_K, dtype=I32) // TOP_K)
    return dest.astype(I32), slot_tok, blk_e, blk_valid, n_used


def kernel(x_prompt, x_sample, cache_k, cache_v, c, c_ctx, norm1_g, norm2_g, w_ada, b_ada, w_in, w_out,
           lam_q1, lam_k1, lam_q2, lam_k2, diff_ln_g, gmlp_norm_g, gmlp_ws, gmlp_bs, pool_w, pool_scale,
           router_w, router_b, exp_w_gu, exp_b_gu, exp_w_down, exp_b_down, final_g):
    nb_ctx, s_ctx, d = x_prompt.shape
    nb_lat, ds, _ = x_sample.shape
    depth = w_in.shape[0]
    n_exp = router_w.shape[-1]
    p_len = cache_k.shape[2]
    t_ctx, t_lat = nb_ctx * s_ctx, nb_lat * ds
    t = t_ctx + t_lat
    assert t_ctx % TILE == 0 and t_lat % TILE == 0 and (t * TOP_K) % TM_MOE == 0

    x_ctx, x_lat = x_prompt.reshape(t_ctx, d), x_sample.reshape(t_lat, d)
    cond_t = jnp.concatenate([c_ctx[None, :], c], axis=0).T
    mod = _ada(cond_t, w_ada, b_ada)
    mod = mod.reshape(depth, 1 + nb_lat, 6, d)

    rope_tabs = _rope_tables(ds)
    ck = cache_k.reshape(nb_lat, depth, p_len, A_HEADS * HEAD_W)
    cv = cache_v.reshape(nb_lat, depth, p_len, A_WIDTH)
    w_in_b = w_in.astype(BF16)
    w_out_b = w_out.astype(BF16)
    ws_b = gmlp_ws.astype(BF16)
    pw_b = pool_w.astype(BF16)
    rw_pad = jnp.pad(router_w, ((0, 0), (0, 0), (0, LANES - n_exp)))
    rw_hi = rw_pad.astype(BF16)
    rw_parts = jnp.stack([rw_hi, (rw_pad - rw_hi.astype(F32)).astype(BF16)], axis=1)
    rb_pad = jnp.pad(router_b, ((0, 0), (0, LANES - n_exp)))
    n_blocks = t * TOP_K // TM_MOE + n_exp

    new_k, new_v = [], []
    for l in range(depth):
        lam_init = 0.8 - 0.6 * math.exp(-0.3 * l)
        lamp = jnp.stack([lam_q1[l], lam_k1[l], lam_q2[l], lam_k2[l]])
        lng = diff_ln_g[l][None, :]
        z = _inproj(x_ctx, x_lat, mod[l], norm1_g[l][None, :], w_in_b[l], ds=ds)
        oa_ctx, k_new, v_new = _attn_ctx(z, lamp, lng, nb=nb_ctx, s=s_ctx, lam_init=lam_init)
        new_k.append(k_new.reshape(nb_ctx, s_ctx, A_HEADS, HEAD_W))
        new_v.append(v_new.reshape(nb_ctx, s_ctx, A_HEADS, V_DIM))
        oa_lat = _attn_lat(z, ck, cv, rope_tabs, lamp, lng, layer=l, t_ctx=t_ctx, nb=nb_lat, ds=ds,
                           lam_init=lam_init)
        gb = jnp.repeat(gmlp_bs[l].T, B_CH, axis=1)
        x_mid, h_packed, logits = _mixpost(
            oa_ctx, oa_lat, z, x_ctx, x_lat, mod[l], norm2_g[l][None, :], w_out_b[l], gmlp_norm_g[l][None, :],
            ws_b[l], gb, pw_b[l], pool_scale[l][None, :], rw_parts[l], rb_pad[l][None, :], s_ctx=s_ctx, ds=ds)
        idx, gates, rank, counts = _route(logits, n_exp)
        dest, slot_tok, blk_e, blk_valid, n_used = _slot_layout(
            idx[:, :TOP_K], rank[:, :TOP_K], counts[0, :n_exp].astype(I32), n_blocks)
        dest_tiles = dest.reshape(t // TILE, 1, TILE * TOP_K)
        yb = _moe(blk_e, blk_valid, n_used, slot_tok, h_packed, exp_w_gu, exp_b_gu, exp_w_down, exp_b_down,
                  layer=l)
        x_ctx, x_lat = _combine(dest_tiles, gates, x_mid, mod[l], final_g[None, :], yb, t_ctx=t_ctx, ds=ds,
                                final=(l == depth - 1))

    return (x_ctx.reshape(nb_ctx, s_ctx, d), x_lat.reshape(nb_lat, ds, d),
            jnp.stack(new_k, axis=1), jnp.stack(new_v, axis=1))
```

```python
import functools
import math

import numpy as np
import jax
import jax.numpy as jnp
from jax import lax
from jax.experimental import pallas as pl
from jax.experimental.pallas import tpu as pltpu

F32 = jnp.float32
BF16 = jnp.bfloat16
I32 = jnp.int32
U32 = jnp.uint32

GRID_W = 64
A_HEADS = 8
QK_DIM = 64
V_DIM = 2 * QK_DIM
HEAD_W = 2 * QK_DIM
Q_W = A_HEADS * HEAD_W
A_WIDTH = A_HEADS * V_DIM
B_GROUPS = 4
CHUNK = 128
B_CH = 128
B_WIDTH = B_GROUPS * B_CH
POOL_WINDOWS = (2, 4, 8, 16)
N_POOL = len(POOL_WINDOWS)
C_CH = 128
C_WIDTH = N_POOL * C_CH
ROPE_BASE = 10000.0
N_FREQ = QK_DIM // 4
TOP_K = 4
SWIGLU_ALPHA = 1.702
SWIGLU_LIMIT = 7.0
EPS = 1e-6
QK_SCALE = QK_DIM ** -0.5

LANES = 128
TILE = 256
HALO = 16
TM_PROJ_CHOICES = (1024, 512, 256)
LOG2E = 1.4426950408889634
TM_MOE = 1024
SUB_MOE = 256
TF_MOE = 256
TQ_ATT = 512
VMEM_LIMIT = 56 * 1024 * 1024


def _cparams(sem, vmem=VMEM_LIMIT):
    return pltpu.CompilerParams(dimension_semantics=sem, vmem_limit_bytes=vmem)


def _norm_mod(x, g, shift, scale):
    ms = jnp.mean(x * x, axis=-1, keepdims=True)
    return x * lax.rsqrt(ms + EPS) * g * (1.0 + scale) + shift


def _rms_lanes(x):
    return x * lax.rsqrt(jnp.mean(x * x, axis=-1, keepdims=True) + EPS)


def _dot(a, b):
    return jnp.dot(a, b, preferred_element_type=F32)


def _dot_nt(a, b):
    return lax.dot_general(a, b, (((1,), (1,)), ((), ())), preferred_element_type=F32)


def _ada_body(ct_ref, w_ref, b_ref, o_ref, sb, *, kc):
    d, ng = ct_ref.shape
    tn = w_ref.shape[-1]

    @pl.when(jnp.logical_and(pl.program_id(0) == 0, pl.program_id(1) == 0))
    def _():
        def fill(c, carry):
            k0 = pl.multiple_of(c * kc, kc)
            s = jax.nn.silu(ct_ref[pl.ds(k0, kc), :])
            for gi in range(ng):
                sb[gi, pl.ds(k0, kc), :] = jnp.broadcast_to(s[:, gi:gi + 1], (kc, LANES))
            return carry

        lax.fori_loop(0, d // kc, fill, 0)

    def body(c, accs):
        k0 = pl.multiple_of(c * kc, kc)
        w = w_ref[0, pl.ds(k0, kc), :]
        out = []
        for gi, acc in enumerate(accs):
            s = sb[gi, pl.ds(k0, kc), :]
            parts = [jnp.sum((w[:, t * LANES:(t + 1) * LANES] * s).reshape(kc // 8, 8, LANES), axis=0)
                     for t in range(tn // LANES)]
            out.append(acc + jnp.concatenate(parts, axis=1))
        return tuple(out)

    accs = lax.fori_loop(0, d // kc, body, tuple(jnp.zeros((8, tn), F32) for _ in range(ng)))
    for gi in range(ng):
        o_ref[0, gi:gi + 1, :] = jnp.sum(accs[gi], axis=0, keepdims=True) + b_ref[0]


def _ada(cond_t, w_ada, b_ada):
    d, ng = cond_t.shape
    nl, _, n = w_ada.shape
    tn = 1024
    return pl.pallas_call(
        functools.partial(_ada_body, kc=32),
        grid=(nl, n // tn),
        in_specs=[
            pl.BlockSpec((d, ng), lambda l, j: (0, 0)),
            pl.BlockSpec((1, d, tn), lambda l, j: (l, 0, j)),
            pl.BlockSpec((1, 1, tn), lambda l, j: (l, 0, j)),
        ],
        out_specs=pl.BlockSpec((1, ng, tn), lambda l, j: (l, 0, j)),
        out_shape=jax.ShapeDtypeStruct((nl, ng, n), F32),
        scratch_shapes=[pltpu.VMEM((ng, d, LANES), F32)],
        compiler_params=_cparams(("arbitrary", "arbitrary")),
        name="ada",
    )(cond_t, w_ada, b_ada.reshape(nl, 1, n))


def _inproj_body(xc_ref, xl_ref, mod_ref, g_ref, w_ref, z_ref, h_scr, *, rc, n_ctx_blocks):
    @pl.when(pl.program_id(1) == 0)
    def _():
        shift = mod_ref[0, 0:1, :]
        scale = mod_ref[0, 1:2, :]
        g = g_ref[...]

        def fill(x_ref):
            def body(c, carry):
                r0 = pl.multiple_of(c * rc, rc)
                h = _norm_mod(x_ref[pl.ds(r0, rc), :], g, shift, scale)
                h_scr[pl.ds(r0, rc), :] = h.astype(BF16)
                return carry

            lax.fori_loop(0, x_ref.shape[0] // rc, body, 0)

        is_ctx = pl.program_id(0) < n_ctx_blocks
        pl.when(is_ctx)(lambda: fill(xc_ref))
        pl.when(jnp.logical_not(is_ctx))(lambda: fill(xl_ref))

    z_ref[...] = _dot(h_scr[...], w_ref[...])


def _group_of_block(i, rows, t_ctx, ds):
    r = i * rows
    return jnp.where(r < t_ctx, 0, 1 + (r - t_ctx) // ds)


def _pair_specs(rows, width, n_ctx_blocks):
    return [
        pl.BlockSpec((rows, width), lambda i, *_: (jnp.minimum(i, n_ctx_blocks - 1), 0)),
        pl.BlockSpec((rows, width), lambda i, *_: (jnp.maximum(i - n_ctx_blocks, 0), 0)),
    ]


def _inproj(x_ctx, x_lat, mod, g1, w_in, *, ds):
    t_ctx, d = x_ctx.shape
    t = t_ctx + x_lat.shape[0]
    n = w_in.shape[1]
    tm = next(m for m in TM_PROJ_CHOICES if t_ctx % m == 0 and ds % m == 0)
    tn = 768
    assert n % tn == 0
    grp = functools.partial(_group_of_block, rows=tm, t_ctx=t_ctx, ds=ds)
    return pl.pallas_call(
        functools.partial(_inproj_body, rc=64, n_ctx_blocks=t_ctx // tm),
        grid=(t // tm, n // tn),
        in_specs=_pair_specs(tm, d, t_ctx // tm) + [
            pl.BlockSpec((1, 6, d), lambda i, j: (grp(i), 0, 0)),
            pl.BlockSpec((1, d), lambda i, j: (0, 0)),
            pl.BlockSpec((d, tn), lambda i, j: (0, j)),
        ],
        out_specs=pl.BlockSpec((tm, tn), lambda i, j: (i, j)),
        out_shape=jax.ShapeDtypeStruct((t, n), F32),
        scratch_shapes=[pltpu.VMEM((tm, d), BF16)],
        compiler_params=_cparams(("arbitrary", "arbitrary")),
        name="inproj",
    )(x_ctx, x_lat, mod, g1, w_in)


def _lam_value(lamp_ref, lam_init):
    lp = lamp_ref[...]
    a = jnp.sum(lp[0:1] * lp[1:2], keepdims=True)
    b = jnp.sum(lp[2:3] * lp[3:4], keepdims=True)
    return jnp.exp(a) - jnp.exp(b) + lam_init


def _split_halves(q):
    first = lax.broadcasted_iota(I32, (1, HEAD_W), 1) < QK_DIM
    return jnp.where(first, q, 0.0).astype(BF16), jnp.where(first, 0.0, q).astype(BF16)


def _attn_ctx_body(lamp_ref, lng_ref, q_ref, k_ref, v_ref, o_ref, ko_ref, vo_ref, *, lam_init):
    lam = _lam_value(lamp_ref, lam_init)
    g = lng_ref[...] * (1.0 - lam_init)
    ko_ref[...] = k_ref[...]
    vo_ref[...] = v_ref[...]
    for h in range(A_HEADS):
        hs = slice(h * HEAD_W, (h + 1) * HEAD_W)
        k = k_ref[:, hs].astype(BF16)
        v = v_ref[:, hs].astype(BF16)
        ps = []
        for qh in _split_halves(q_ref[:, hs] * QK_SCALE):
            s = _dot_nt(qh, k)
            e = jnp.exp(s - jnp.max(s, axis=-1, keepdims=True))
            ps.append(e * (1.0 / jnp.sum(e, axis=-1, keepdims=True)))
        o = _dot((ps[0] - lam * ps[1]).astype(BF16), v)
        o_ref[:, hs] = (_rms_lanes(o) * g).astype(o_ref.dtype)


def _attn_ctx(z, lamp, lng, *, nb, s, lam_init):
    return pl.pallas_call(
        functools.partial(_attn_ctx_body, lam_init=lam_init),
        grid=(nb,),
        in_specs=[
            pl.BlockSpec(lamp.shape, lambda b: (0, 0)),
            pl.BlockSpec((1, V_DIM), lambda b: (0, 0)),
            pl.BlockSpec((s, Q_W), lambda b: (b, 0)),
            pl.BlockSpec((s, Q_W), lambda b: (b, 1)),
            pl.BlockSpec((s, A_WIDTH), lambda b: (b, 2)),
        ],
        out_specs=[pl.BlockSpec((s, A_WIDTH), lambda b: (b, 0))] * 3,
        out_shape=[
            jax.ShapeDtypeStruct((nb * s, A_WIDTH), BF16),
            jax.ShapeDtypeStruct((nb * s, Q_W), F32),
            jax.ShapeDtypeStruct((nb * s, A_WIDTH), F32),
        ],
        compiler_params=_cparams(("parallel",)),
        name="attn_ctx",
    )(lamp, lng, z, z, z)


def _rope(x, c, sa, sb):
    return x * c + pltpu.roll(x, HEAD_W - N_FREQ, 1) * sa + pltpu.roll(x, N_FREQ, 1) * sb


def _attn_lat_body(lamp_ref, lng_ref, q_ref, k_ref, v_ref, ck_ref, cv_ref,
                   cq_ref, saq_ref, sbq_ref, ckk_ref, sak_ref, sbk_ref,
                   o_ref, kall, vall, s_scr, e_scr, *, lam_init, kc, rc):
    ds = k_ref.shape[0]
    p = ck_ref.shape[2]
    tq = q_ref.shape[0]
    nch = (ds + p) // kc

    @pl.when(pl.program_id(2) == 0)
    def _():
        def ones_col(rows):
            return jnp.where(lax.broadcasted_iota(I32, (rows, LANES), 1) == 0, 1.0, 0.0).astype(BF16)

        def body(c, carry):
            r0 = pl.multiple_of(c * rc, rc)
            rs = pl.ds(r0, rc)
            kall[rs, :] = _rope(k_ref[rs, :], ckk_ref[rs, :], sak_ref[rs, :], sbk_ref[rs, :]).astype(BF16)
            vall[rs, 0:V_DIM] = v_ref[rs, :].astype(BF16)
            vall[rs, V_DIM:] = ones_col(rc)
            return carry

        lax.fori_loop(0, ds // rc, body, 0)
        kall[ds:ds + p, :] = ck_ref[0, 0].astype(BF16)
        vall[ds:ds + p, 0:V_DIM] = cv_ref[0, 0].astype(BF16)
        vall[ds:ds + p, V_DIM:] = ones_col(p)

    lam = _lam_value(lamp_ref, lam_init)
    q = _rope(q_ref[...], cq_ref[...], saq_ref[...], sbq_ref[...]) * (QK_SCALE * LOG2E)
    halves = _split_halves(q)
    chunks = [slice(c * kc, (c + 1) * kc) for c in range(nch)]

    def qk(n, ch, mrun):
        s = _dot_nt(halves[n], kall[ch, :])
        s_scr[n, :, ch] = s
        for t in range(kc // LANES):
            mrun = jnp.maximum(mrun, s[:, t * LANES:(t + 1) * LANES])
        return mrun

    def ex(n, ch, m):
        e_scr[n, :, ch] = jnp.exp2(s_scr[n, :, ch] - m).astype(BF16)

    neg = jnp.full((tq, LANES), -jnp.inf, F32)
    mrun = neg
    for ch in chunks:
        mrun = qk(0, ch, mrun)
    m0 = jnp.max(mrun, axis=-1, keepdims=True)
    mrun = neg
    for ch in chunks:
        mrun = qk(1, ch, mrun)
        ex(0, ch, m0)
    m1 = jnp.max(mrun, axis=-1, keepdims=True)
    oe0 = jnp.zeros((tq, V_DIM + LANES), F32)
    for ch in chunks:
        ex(1, ch, m1)
        oe0 = oe0 + _dot(e_scr[0, :, ch], vall[ch, :])
    oe1 = _dot(e_scr[1], vall[...])
    outs = [oe[:, 0:V_DIM] * (1.0 / oe[:, V_DIM:V_DIM + 1]) for oe in (oe0, oe1)]
    o = outs[0] - lam * outs[1]
    o_ref[...] = (_rms_lanes(o) * (lng_ref[...] * (1.0 - lam_init))).astype(o_ref.dtype)


def _attn_lat(z, cache_k, cache_v, rope_tabs, lamp, lng, *, layer, t_ctx, nb, ds, lam_init):
    p = cache_k.shape[2]
    tq = min(TQ_ATT, ds)
    sk = ds + p
    kc = 512 if sk % 512 == 0 else LANES
    assert t_ctx % ds == 0 and ds % tq == 0 and sk % kc == 0
    seq0, q0 = t_ctx // ds, t_ctx // tq
    nq = ds // tq
    qh, kh, vh = 0, Q_W // HEAD_W, 2 * Q_W // HEAD_W
    tab_q = pl.BlockSpec((tq, HEAD_W), lambda b, h, qi: (qi, 0))
    tab_k = pl.BlockSpec((ds, HEAD_W), lambda b, h, qi: (0, 0))
    cos, sa, sb = rope_tabs
    return pl.pallas_call(
        functools.partial(_attn_lat_body, lam_init=lam_init, kc=kc, rc=256),
        grid=(nb, A_HEADS, nq),
        in_specs=[
            pl.BlockSpec(lamp.shape, lambda b, h, qi: (0, 0)),
            pl.BlockSpec((1, V_DIM), lambda b, h, qi: (0, 0)),
            pl.BlockSpec((tq, HEAD_W), lambda b, h, qi: (q0 + b * nq + qi, qh + h)),
            pl.BlockSpec((ds, HEAD_W), lambda b, h, qi: (seq0 + b, kh + h)),
            pl.BlockSpec((ds, HEAD_W), lambda b, h, qi: (seq0 + b, vh + h)),
            pl.BlockSpec((1, 1, p, HEAD_W), lambda b, h, qi: (b, layer, 0, h)),
            pl.BlockSpec((1, 1, p, V_DIM), lambda b, h, qi: (b, layer, 0, h)),
            tab_q, tab_q, tab_q, tab_k, tab_k, tab_k,
        ],
        out_specs=pl.BlockSpec((tq, V_DIM), lambda b, h, qi: (b * nq + qi, h)),
        out_shape=jax.ShapeDtypeStruct((nb * ds, A_WIDTH), BF16),
        scratch_shapes=[
            pltpu.VMEM((sk, HEAD_W), BF16),
            pltpu.VMEM((sk, V_DIM + LANES), BF16),
            pltpu.VMEM((2, tq, sk), F32),
            pltpu.VMEM((2, tq, sk), BF16),
        ],
        compiler_params=_cparams(("parallel", "parallel", "arbitrary")),
        name="attn_lat",
    )(lamp, lng, z, z, z, cache_k, cache_v, cos, sa, sb, cos, sa, sb)


def _rope_tables(n_tok):
    n_rows = n_tok // GRID_W
    row = jnp.repeat(jnp.arange(n_rows), GRID_W).astype(F32)
    col = jnp.tile(jnp.arange(GRID_W), n_rows).astype(F32)
    inv = 1.0 / (ROPE_BASE ** (jnp.arange(N_FREQ, dtype=F32) / N_FREQ))
    ang = jnp.stack([row[:, None] * inv, col[:, None] * inv], axis=1)
    cos, sin = jnp.cos(ang), jnp.sin(ang)
    zero = jnp.zeros_like(sin)

    def lanes(first, second):
        per_axis = jnp.concatenate([first, second], axis=-1)
        return jnp.tile(per_axis.reshape(n_tok, 2 * 2 * N_FREQ), (1, 2))

    return lanes(cos, cos), lanes(-sin, zero), lanes(zero, sin)


def _pack_pair(hi, lo):
    hb = lax.bitcast_convert_type(hi.astype(BF16).astype(F32), U32)
    lb = lax.bitcast_convert_type(lo.astype(BF16).astype(F32), U32)
    return hb | (lb >> 16)


def _unpack_pair(u):
    hi = lax.bitcast_convert_type(u & jnp.uint32(0xFFFF0000), F32)
    lo = lax.bitcast_convert_type(u << 16, F32)
    return hi.astype(BF16), lo.astype(BF16)


def _hi_lo(x):
    hi = x.astype(BF16)
    return hi, (x - hi.astype(F32)).astype(BF16)


def _mixpost_body(oac_ref, oal_ref, zb_ref, zc_ref, zp_ref, zn_ref, xc_ref, xl_ref, mod_ref, g2_ref,
                  wout_ref, gng_ref, ws_ref, gb_ref, pw_ref, ps_ref, band_ref, bandp_ref, bandn_ref,
                  rw_ref, rb_ref, xo_ref, h_ref, lg_ref, cat_scr, *, n_ctx_tiles, tps_ctx, tps_lat):
    i = pl.program_id(0)
    is_ctx = i < n_ctx_tiles
    pos = jnp.where(is_ctx, i % tps_ctx, (i - n_ctx_tiles) % tps_lat)
    first = pos == 0
    last = pos == jnp.where(is_ctx, tps_ctx, tps_lat) - 1

    cat_scr[:, 0:A_WIDTH] = jnp.where(is_ctx, oac_ref[...], oal_ref[...])

    zb = jax.nn.gelu(zb_ref[...])
    u, v = zb[:, :B_WIDTH], zb[:, B_WIDTH:]
    for g in range(B_GROUPS):
        gs = slice(g * B_CH, (g + 1) * B_CH)
        vn = (_rms_lanes(v[:, gs]) * gng_ref[:, gs]).astype(BF16)
        for n in range(TILE // CHUNK):
            rs = slice(n * CHUNK, (n + 1) * CHUNK)
            mixed = _dot(ws_ref[g], vn[rs]) + gb_ref[:, gs]
            cat_scr[rs, A_WIDTH + g * B_CH:A_WIDTH + (g + 1) * B_CH] = (u[rs, gs] * mixed).astype(BF16)

    xc = zc_ref[...]
    xc_parts = _hi_lo(xc)
    xp_parts = _hi_lo(zp_ref[...] * jnp.where(first, 0.0, 1.0))
    xn_parts = _hi_lo(zn_ref[...] * jnp.where(last, 0.0, 1.0))
    r = lax.broadcasted_iota(I32, (TILE, 1), 0)
    for g, w in enumerate(POOL_WINDOWS):
        gs = slice(g * C_CH, (g + 1) * C_CH)
        half = w // 2
        acc = jnp.zeros((TILE, C_CH), F32)
        for part in xc_parts:
            acc = acc + _dot(band_ref[g], part[:, gs])
        for part in xp_parts:
            acc = acc + _dot(bandp_ref[g], part[:, gs])
        for part in xn_parts:
            acc = acc + _dot(bandn_ref[g], part[:, gs])
        left = jnp.where(first, jnp.minimum(half, r), half)
        right = jnp.where(last, jnp.minimum(half - 1, TILE - 1 - r), half - 1)
        cnt = (left + right + 1).astype(F32)
        pooled = acc / cnt - xc[:, gs]
        y = _dot(pooled.astype(BF16), pw_ref[g]) * ps_ref[:, gs]
        c0 = A_WIDTH + B_WIDTH + g * C_CH
        cat_scr[:, c0:c0 + C_CH] = y.astype(BF16)

    mix = _dot(cat_scr[...], wout_ref[...])
    x_new = jnp.where(is_ctx, xc_ref[...], xl_ref[...]) + mod_ref[0, 2:3, :] * mix
    xo_ref[...] = x_new
    h2 = _norm_mod(x_new, g2_ref[...], mod_ref[0, 3:4, :], mod_ref[0, 4:5, :])
    half_d = h2.shape[1] // 2
    h_ref[...] = _pack_pair(h2[:, :half_d], h2[:, half_d:])
    h_hi, h_lo = _hi_lo(h2)
    lg_ref[...] = _dot(h_hi, rw_ref[0]) + (_dot(h_lo, rw_ref[0]) + _dot(h_hi, rw_ref[1])) + rb_ref[...]


def _band_matrices():
    r = np.arange(TILE)[:, None]
    cur, prev, nxt = [], [], []
    for w in POOL_WINDOWS:
        half = w // 2
        c = np.arange(TILE)[None, :]
        cur.append((c >= r - half) & (c <= r + half - 1))
        ch = np.arange(HALO)[None, :]
        prev.append(ch - HALO >= r - half)
        nxt.append(TILE + ch <= r + half - 1)
    to = lambda m: jnp.asarray(np.stack(m).astype(np.float32), dtype=BF16)
    return to(cur), to(prev), to(nxt)


def _mixpost(oa_ctx, oa_lat, z, x_ctx, x_lat, mod, g2, w_out, gng, ws, gb, pw, ps, rw, rb, *, s_ctx, ds):
    t_ctx, d = x_ctx.shape
    t = t_ctx + x_lat.shape[0]
    nt = t // TILE
    n_ctx_tiles = t_ctx // TILE
    tps_ctx, tps_lat = s_ctx // TILE, ds // TILE
    assert s_ctx % TILE == 0 and ds % TILE == 0
    grp = functools.partial(_group_of_block, rows=TILE, t_ctx=t_ctx, ds=ds)
    band, bandp, bandn = _band_matrices()
    zb_blk = (2 * Q_W + A_WIDTH) // (2 * B_WIDTH)
    zc_blk = (2 * Q_W + A_WIDTH + 2 * B_WIDTH) // C_WIDTH
    hpt = TILE // HALO
    const2 = lambda i: (0, 0)
    const3 = lambda i: (0, 0, 0)
    return pl.pallas_call(
        functools.partial(_mixpost_body, n_ctx_tiles=n_ctx_tiles, tps_ctx=tps_ctx, tps_lat=tps_lat),
        grid=(nt,),
        in_specs=_pair_specs(TILE, A_WIDTH, n_ctx_tiles) + [
            pl.BlockSpec((TILE, 2 * B_WIDTH), lambda i: (i, zb_blk)),
            pl.BlockSpec((TILE, C_WIDTH), lambda i: (i, zc_blk)),
            pl.BlockSpec((HALO, C_WIDTH), lambda i: (jnp.maximum(i * hpt - 1, 0), zc_blk)),
            pl.BlockSpec((HALO, C_WIDTH), lambda i: (jnp.minimum((i + 1) * hpt, nt * hpt - 1), zc_blk)),
        ] + _pair_specs(TILE, d, n_ctx_tiles) + [
            pl.BlockSpec((1, 6, d), lambda i: (grp(i), 0, 0)),
            pl.BlockSpec((1, d), const2),
            pl.BlockSpec(w_out.shape, const2),
            pl.BlockSpec((1, B_WIDTH), const2),
            pl.BlockSpec(ws.shape, const3),
            pl.BlockSpec(gb.shape, const2),
            pl.BlockSpec(pw.shape, const3),
            pl.BlockSpec((1, C_WIDTH), const2),
            pl.BlockSpec(band.shape, const3),
            pl.BlockSpec(bandp.shape, const3),
            pl.BlockSpec(bandn.shape, const3),
            pl.BlockSpec(rw.shape, const3),
            pl.BlockSpec(rb.shape, const2),
        ],
        out_specs=[
            pl.BlockSpec((TILE, d), lambda i: (i, 0)),
            pl.BlockSpec((TILE, d // 2), lambda i: (i, 0)),
            pl.BlockSpec((TILE, LANES), lambda i: (i, 0)),
        ],
        out_shape=[
            jax.ShapeDtypeStruct((t, d), F32),
            jax.ShapeDtypeStruct((t, d // 2), U32),
            jax.ShapeDtypeStruct((t, LANES), F32),
        ],
        scratch_shapes=[pltpu.VMEM((TILE, d), BF16)],
        compiler_params=_cparams(("arbitrary",)),
        name="mixpost",
    )(oa_ctx, oa_lat, z, z, z, z, x_ctx, x_lat, mod, g2, w_out, gng, ws, gb, pw, ps, band, bandp, bandn, rw, rb)


def _route_body(lg_ref, tri_ref, idx_ref, gate_ref, rank_ref, cnt_ref, carry, *, n_exp):
    @pl.when(pl.program_id(0) == 0)
    def _():
        carry[...] = jnp.zeros_like(carry)

    lane = lax.broadcasted_iota(I32, lg_ref.shape, 1).astype(F32)
    l = jnp.where(lane < n_exp, lg_ref[...], -jnp.inf)
    vals, idxs, sels = [], [], []
    for _ in range(TOP_K):
        m = jnp.max(l, axis=-1, keepdims=True)
        idx = jnp.min(jnp.where(l == m, lane, float(LANES)), axis=-1, keepdims=True)
        sel = lane == idx
        vals.append(m)
        idxs.append(idx)
        sels.append(sel)
        l = jnp.where(sel, -jnp.inf, l)
    chosen = functools.reduce(jnp.logical_or, sels)
    chosen_f = jnp.where(chosen, 1.0, 0.0)
    prefix = carry[...] + _dot(tri_ref[...], chosen_f.astype(BF16))
    es = [jnp.exp(v - vals[0]) for v in vals]
    tot = functools.reduce(jnp.add, es)
    idx_o = jnp.zeros(lg_ref.shape, F32)
    gate_o = jnp.zeros(lg_ref.shape, F32)
    rank_o = jnp.zeros(lg_ref.shape, F32)
    for k in range(TOP_K):
        rank_k = jnp.sum(jnp.where(sels[k], prefix, 0.0), axis=-1, keepdims=True)
        idx_o = jnp.where(lane == k, idxs[k], idx_o)
        gate_o = jnp.where(lane == k, es[k] / tot, gate_o)
        rank_o = jnp.where(lane == k, rank_k, rank_o)
    idx_ref[...] = idx_o.astype(I32)
    gate_ref[...] = gate_o
    rank_ref[...] = rank_o.astype(I32)
    carry[...] = carry[...] + jnp.sum(chosen_f, axis=0, keepdims=True)
    cnt_ref[...] = carry[...]


def _route(logits, n_exp):
    t = logits.shape[0]
    tri = jnp.asarray(np.tril(np.ones((TILE, TILE), np.float32), -1), dtype=BF16)
    blk = pl.BlockSpec((TILE, LANES), lambda i: (i, 0))
    return pl.pallas_call(
        functools.partial(_route_body, n_exp=n_exp),
        grid=(t // TILE,),
        in_specs=[blk, pl.BlockSpec((TILE, TILE), lambda i: (0, 0))],
        out_specs=[blk, blk, blk, pl.BlockSpec((1, LANES), lambda i: (0, 0))],
        out_shape=[
            jax.ShapeDtypeStruct((t, LANES), I32),
            jax.ShapeDtypeStruct((t, LANES), F32),
            jax.ShapeDtypeStruct((t, LANES), I32),
            jax.ShapeDtypeStruct((1, LANES), F32),
        ],
        scratch_shapes=[pltpu.VMEM((1, LANES), F32)],
        compiler_params=_cparams(("arbitrary",)),
        name="route",
    )(logits, tri)


DMA_UNROLL = 8


def _dispatch_body(fill_ref, dest_ref, h_ref, xb_ref, zbuf, sem, zsem, *, n_exp):
    rows = h_ref.shape[0]

    @pl.when(pl.program_id(0) == 0)
    def _():
        zbuf[...] = jnp.zeros_like(zbuf)

        def fill_copy(e):
            start = pl.multiple_of(fill_ref[0, e], SUB_MOE)
            return pltpu.make_async_copy(zbuf, xb_ref.at[pl.ds(start, SUB_MOE)], zsem)

        for e in range(n_exp):
            pl.when(fill_ref[1, e] > 0)(lambda e=e: fill_copy(e).start())
        for e in range(n_exp):
            pl.when(fill_ref[1, e] > 0)(lambda e=e: fill_copy(e).wait())

    def row_copy(r, d):
        return pltpu.make_async_copy(h_ref.at[pl.ds(r, 1)], xb_ref.at[pl.ds(d, 1)], sem)

    def for_rows(fn):
        def body(g, carry):
            r0 = pl.multiple_of(g * DMA_UNROLL, DMA_UNROLL)
            for u in range(DMA_UNROLL):
                for k in range(TOP_K):
                    fn(row_copy(r0 + u, dest_ref[0, 0, (r0 + u) * TOP_K + k]))
            return carry

        lax.fori_loop(0, rows // DMA_UNROLL, body, 0)

    for_rows(lambda cp: cp.start())
    for_rows(lambda cp: cp.wait())


def _dispatch(fill, dest_tiles, h_packed, n_slots):
    t, w = h_packed.shape
    return pl.pallas_call(
        functools.partial(_dispatch_body, n_exp=fill.shape[1]),
        grid=(t // TILE,),
        in_specs=[
            pl.BlockSpec(memory_space=pltpu.SMEM),
            pl.BlockSpec((1, 1, TILE * TOP_K), lambda i: (i, 0, 0), memory_space=pltpu.SMEM),
            pl.BlockSpec((TILE, w), lambda i: (i, 0)),
        ],
        out_specs=pl.BlockSpec(memory_space=pl.ANY),
        out_shape=jax.ShapeDtypeStruct((n_slots, w), U32),
        scratch_shapes=[pltpu.VMEM((SUB_MOE, w), U32), pltpu.SemaphoreType.DMA(()), pltpu.SemaphoreType.DMA(())],
        compiler_params=_cparams(("arbitrary",)),
        name="dispatch",
    )(fill, dest_tiles, h_packed)


def _moe_body(be_ref, bv_ref, nu_ref, xb_ref, wg_ref, wu_ref, wd_ref, bg_ref, bu_ref, bd_ref,
              y_ref, x_scr):
    del be_ref, nu_ref
    i, j = pl.program_id(0), pl.program_id(1)
    valid = bv_ref[i]
    tm = xb_ref.shape[0]
    half_d = xb_ref.shape[1]
    n_sub = (valid + SUB_MOE - 1) // SUB_MOE

    @pl.when(valid > 0)
    def _():
        @pl.when(j == 0)
        def _():
            def unpack(s, carry):
                rs = pl.ds(pl.multiple_of(s * SUB_MOE, SUB_MOE), SUB_MOE)
                hi, lo = _unpack_pair(xb_ref[rs, :])
                x_scr[rs, 0:half_d] = hi
                x_scr[rs, half_d:] = lo
                return carry

            def init(s, carry):
                rs = pl.ds(pl.multiple_of(s * SUB_MOE, SUB_MOE), SUB_MOE)
                y_ref[rs, :] = jnp.broadcast_to(bd_ref[0, 0], (SUB_MOE, y_ref.shape[1]))
                return carry

            lax.fori_loop(0, n_sub, unpack, 0)
            lax.fori_loop(0, tm // SUB_MOE, init, 0)

        def ffn(rows):
            x = x_scr[0:rows, :]
            g = _dot(x, wg_ref[0, 0].astype(BF16)) + bg_ref[0, 0]
            u = _dot(x, wu_ref[0, 0].astype(BF16)) + bu_ref[0, 0]
            g = jnp.minimum(g, SWIGLU_LIMIT)
            u = jnp.clip(u, -SWIGLU_LIMIT, SWIGLU_LIMIT)
            a = g * jax.nn.sigmoid(SWIGLU_ALPHA * g) * (u + 1.0)
            y_ref[0:rows, :] += _dot(a.astype(BF16), wd_ref[0, 0].astype(BF16))

        for k in range(1, tm // SUB_MOE + 1):
            pl.when(n_sub == k)(functools.partial(ffn, k * SUB_MOE))


def _moe(blk_e, blk_valid, n_used, xb, w_gu, b_gu, w_dn, b_dn, *, layer):
    n_slots, half_d = xb.shape
    d = 2 * half_d
    n_exp, _, two_f = w_gu.shape[1:]
    f = two_f // 2
    tm, tf = TM_MOE, TF_MOE
    nf = f // tf
    nb = n_slots // tm

    def live(i, nu):
        return i < nu[0]

    def row_blk(i, j, be, bv, nu):
        return (jnp.minimum(i, nu[0] - 1), 0)

    def jj(i, j, nu):
        return jnp.where(live(i, nu), j, nf - 1)

    grid_spec = pltpu.PrefetchScalarGridSpec(
        num_scalar_prefetch=3,
        grid=(nb, nf),
        in_specs=[
            pl.BlockSpec((tm, half_d), row_blk),
            pl.BlockSpec((1, 1, d, tf), lambda i, j, be, bv, nu: (layer, be[i], 0, jj(i, j, nu))),
            pl.BlockSpec((1, 1, d, tf), lambda i, j, be, bv, nu: (layer, be[i], 0, nf + jj(i, j, nu))),
            pl.BlockSpec((1, 1, tf, d), lambda i, j, be, bv, nu: (layer, be[i], jj(i, j, nu), 0)),
            pl.BlockSpec((1, 1, 1, tf), lambda i, j, be, bv, nu: (layer, be[i], 0, jj(i, j, nu))),
            pl.BlockSpec((1, 1, 1, tf), lambda i, j, be, bv, nu: (layer, be[i], 0, nf + jj(i, j, nu))),
            pl.BlockSpec((1, 1, 1, d), lambda i, j, be, bv, nu: (layer, be[i], 0, 0)),
        ],
        out_specs=pl.BlockSpec((tm, d), row_blk),
        scratch_shapes=[pltpu.VMEM((tm, d), BF16)],
    )
    nl = w_gu.shape[0]
    return pl.pallas_call(
        _moe_body,
        grid_spec=grid_spec,
        out_shape=jax.ShapeDtypeStruct((n_slots, d), F32),
        compiler_params=_cparams(("arbitrary", "arbitrary")),
        name="moe",
    )(blk_e, blk_valid, n_used, xb, w_gu, w_gu, w_dn,
      b_gu.reshape(nl, n_exp, 1, two_f), b_gu.reshape(nl, n_exp, 1, two_f), b_dn.reshape(nl, n_exp, 1, d))


def _combine_body(dest_ref, gate_ref, x_ref, mod_ref, fg_ref, yb_ref, oc_ref, ol_ref, buf, sem,
                  *, final, n_ctx_tiles):
    rows = x_ref.shape[0]

    def row_copy(r, k, d):
        return pltpu.make_async_copy(yb_ref.at[pl.ds(d, 1)], buf.at[k, pl.ds(r, 1)], sem)

    def for_rows(fn):
        def body(g, carry):
            r0 = pl.multiple_of(g * DMA_UNROLL, DMA_UNROLL)
            for u in range(DMA_UNROLL):
                for k in range(TOP_K):
                    fn(row_copy(r0 + u, k, dest_ref[0, 0, (r0 + u) * TOP_K + k]))
            return carry

        lax.fori_loop(0, rows // DMA_UNROLL, body, 0)

    for_rows(lambda cp: cp.start())
    for_rows(lambda cp: cp.wait())

    ff = gate_ref[:, 0:1] * buf[0]
    for k in range(1, TOP_K):
        ff = ff + gate_ref[:, k:k + 1] * buf[k]
    x_new = x_ref[...] + mod_ref[0, 5:6, :] * ff
    if final:
        x_new = _rms_lanes(x_new) * fg_ref[...]
    is_ctx = pl.program_id(0) < n_ctx_tiles

    @pl.when(is_ctx)
    def _():
        oc_ref[...] = x_new

    @pl.when(jnp.logical_not(is_ctx))
    def _():
        ol_ref[...] = x_new


def _combine(dest_tiles, gates, x, mod, final_g, yb, *, t_ctx, ds, final):
    t, d = x.shape
    n_ctx_tiles = t_ctx // TILE
    grp = functools.partial(_group_of_block, rows=TILE, t_ctx=t_ctx, ds=ds)
    return pl.pallas_call(
        functools.partial(_combine_body, final=final, n_ctx_tiles=n_ctx_tiles),
        grid=(t // TILE,),
        in_specs=[
            pl.BlockSpec((1, 1, TILE * TOP_K), lambda i: (i, 0, 0), memory_space=pltpu.SMEM),
            pl.BlockSpec((TILE, LANES), lambda i: (i, 0)),
            pl.BlockSpec((TILE, d), lambda i: (i, 0)),
            pl.BlockSpec((1, 6, d), lambda i: (grp(i), 0, 0)),
            pl.BlockSpec((1, d), lambda i: (0, 0)),
            pl.BlockSpec(memory_space=pl.ANY),
        ],
        out_specs=_pair_specs(TILE, d, n_ctx_tiles),
        out_shape=[jax.ShapeDtypeStruct((t_ctx, d), F32), jax.ShapeDtypeStruct((t - t_ctx, d), F32)],
        scratch_shapes=[pltpu.VMEM((TOP_K, TILE, d), F32), pltpu.SemaphoreType.DMA(())],
        compiler_params=_cparams(("arbitrary",)),
        name="combine",
    )(dest_tiles, gates, x, mod, final_g, yb)


def _slot_layout(idx, rank, counts, n_blocks):
    n_exp = counts.shape[0]
    n_blk = (counts + TM_MOE - 1) // TM_MOE
    padded = n_blk * TM_MOE
    pend = jnp.cumsum(padded)
    pstart = pend - padded
    per = jnp.maximum(-(-counts // jnp.maximum(n_blk, 1)), 1)
    per = (per + SUB_MOE - 1) // SUB_MOE * SUB_MOE
    dest = pstart[idx] + (rank // per[idx]) * TM_MOE + rank % per[idx]
    bstart = jnp.arange(n_blocks, dtype=I32) * TM_MOE
    n_used = (pend[-1] // TM_MOE).astype(I32).reshape(1)
    owner_of = jnp.minimum(bstart, jnp.maximum(pend[-1] - TM_MOE, 0))
    blk_e = jnp.minimum(jnp.searchsorted(pend, owner_of, side='right'), n_exp - 1).astype(I32)
    local = (bstart - pstart[blk_e]) // TM_MOE
    blk_valid = jnp.clip(counts[blk_e] - local * per[blk_e], 0, per[blk_e])
    blk_valid = jnp.where(bstart < pend[-1], blk_valid, 0).astype(I32)
    last_blk = jnp.maximum(counts - 1, 0) // per
    last_cnt = counts - last_blk * per
    last_group = pstart + last_blk * TM_MOE + jnp.maximum((last_cnt + SUB_MOE - 1) // SUB_MOE - 1, 0) * SUB_MOE
    fill = jnp.stack([last_group, (counts > 0).astype(I32)]).astype(I32)
    return dest.astype(I32), fill, blk_e, blk_valid, n_used


def kernel(x_prompt, x_sample, cache_k, cache_v, c, c_ctx, norm1_g, norm2_g, w_ada, b_ada, w_in, w_out,
           lam_q1, lam_k1, lam_q2, lam_k2, diff_ln_g, gmlp_norm_g, gmlp_ws, gmlp_bs, pool_w, pool_scale,
           router_w, router_b, exp_w_gu, exp_b_gu, exp_w_down, exp_b_down, final_g):
    nb_ctx, s_ctx, d = x_prompt.shape
    nb_lat, ds, _ = x_sample.shape
    depth = w_in.shape[0]
    n_exp = router_w.shape[-1]
    p_len = cache_k.shape[2]
    t_ctx, t_lat = nb_ctx * s_ctx, nb_lat * ds
    t = t_ctx + t_lat
    assert t_ctx % TILE == 0 and t_lat % TILE == 0 and (t * TOP_K) % TM_MOE == 0

    x_ctx, x_lat = x_prompt.reshape(t_ctx, d), x_sample.reshape(t_lat, d)
    cond_t = jnp.concatenate([c_ctx[None, :], c], axis=0).T
    mod = _ada(cond_t, w_ada, b_ada)
    mod = mod.reshape(depth, 1 + nb_lat, 6, d)

    rope_tabs = _rope_tables(ds)
    ck = cache_k.reshape(nb_lat, depth, p_len, A_HEADS * HEAD_W)
    cv = cache_v.reshape(nb_lat, depth, p_len, A_WIDTH)
    w_in_b = w_in.astype(BF16)
    w_out_b = w_out.astype(BF16)
    ws_b = gmlp_ws.astype(BF16)
    pw_b = pool_w.astype(BF16)
    rw_pad = jnp.pad(router_w, ((0, 0), (0, 0), (0, LANES - n_exp)))
    rw_hi = rw_pad.astype(BF16)
    rw_parts = jnp.stack([rw_hi, (rw_pad - rw_hi.astype(F32)).astype(BF16)], axis=1)
    rb_pad = jnp.pad(router_b, ((0, 0), (0, LANES - n_exp)))
    n_blocks = t * TOP_K // TM_MOE + n_exp

    new_k, new_v = [], []
    for l in range(depth):
        lam_init = 0.8 - 0.6 * math.exp(-0.3 * l)
        lamp = jnp.stack([lam_q1[l], lam_k1[l], lam_q2[l], lam_k2[l]])
        lng = diff_ln_g[l][None, :]
        z = _inproj(x_ctx, x_lat, mod[l], norm1_g[l][None, :], w_in_b[l], ds=ds)
        oa_ctx, k_new, v_new = _attn_ctx(z, lamp, lng, nb=nb_ctx, s=s_ctx, lam_init=lam_init)
        new_k.append(k_new.reshape(nb_ctx, s_ctx, A_HEADS, HEAD_W))
        new_v.append(v_new.reshape(nb_ctx, s_ctx, A_HEADS, V_DIM))
        oa_lat = _attn_lat(z, ck, cv, rope_tabs, lamp, lng, layer=l, t_ctx=t_ctx, nb=nb_lat, ds=ds,
                           lam_init=lam_init)
        gb = jnp.repeat(gmlp_bs[l].T, B_CH, axis=1)
        x_mid, h_packed, logits = _mixpost(
            oa_ctx, oa_lat, z, x_ctx, x_lat, mod[l], norm2_g[l][None, :], w_out_b[l], gmlp_norm_g[l][None, :],
            ws_b[l], gb, pw_b[l], pool_scale[l][None, :], rw_parts[l], rb_pad[l][None, :], s_ctx=s_ctx, ds=ds)
        idx, gates, rank, counts = _route(logits, n_exp)
        dest, fill, blk_e, blk_valid, n_used = _slot_layout(
            idx[:, :TOP_K], rank[:, :TOP_K], counts[0, :n_exp].astype(I32), n_blocks)
        dest_tiles = dest.reshape(t // TILE, 1, TILE * TOP_K)
        xb = _dispatch(fill, dest_tiles, h_packed, n_blocks * TM_MOE)
        yb = _moe(blk_e, blk_valid, n_used, xb, exp_w_gu, exp_b_gu, exp_w_down, exp_b_down, layer=l)
        x_ctx, x_lat = _combine(dest_tiles, gates, x_mid, mod[l], final_g[None, :], yb, t_ctx=t_ctx, ds=ds,
                                final=(l == depth - 1))

    return (x_ctx.reshape(nb_ctx, s_ctx, d), x_lat.reshape(nb_lat, ds, d),
            jnp.stack(new_k, axis=1), jnp.stack(new_v, axis=1))
```

```python
import functools
import math

import numpy as np
import jax
import jax.numpy as jnp
from jax import lax
from jax.experimental import pallas as pl
from jax.experimental.pallas import tpu as pltpu

F32 = jnp.float32
BF16 = jnp.bfloat16
I32 = jnp.int32
U32 = jnp.uint32

GRID_W = 64
A_HEADS = 8
QK_DIM = 64
V_DIM = 2 * QK_DIM
HEAD_W = 2 * QK_DIM
Q_W = A_HEADS * HEAD_W
A_WIDTH = A_HEADS * V_DIM
B_GROUPS = 4
CHUNK = 128
B_CH = 128
B_WIDTH = B_GROUPS * B_CH
POOL_WINDOWS = (2, 4, 8, 16)
N_POOL = len(POOL_WINDOWS)
C_CH = 128
C_WIDTH = N_POOL * C_CH
ROPE_BASE = 10000.0
N_FREQ = QK_DIM // 4
TOP_K = 4
SWIGLU_ALPHA = 1.702
SWIGLU_LIMIT = 7.0
EPS = 1e-6
QK_SCALE = QK_DIM ** -0.5

LANES = 128
TILE = 256
HALO = 16
TM_PROJ_CHOICES = (1024, 512, 256)
LOG2E = 1.4426950408889634
TM_MOE = 1024
SUB_MOE = 256
TF_MOE = 256
TQ_ATT = 512
VMEM_LIMIT = 56 * 1024 * 1024


def _cparams(sem, vmem=VMEM_LIMIT):
    return pltpu.CompilerParams(dimension_semantics=sem, vmem_limit_bytes=vmem)


def _norm_mod(x, g, shift, scale):
    ms = jnp.mean(x * x, axis=-1, keepdims=True)
    return x * lax.rsqrt(ms + EPS) * g * (1.0 + scale) + shift


def _rms_lanes(x):
    return x * lax.rsqrt(jnp.mean(x * x, axis=-1, keepdims=True) + EPS)


def _dot(a, b):
    return jnp.dot(a, b, preferred_element_type=F32)


def _dot_nt(a, b):
    return lax.dot_general(a, b, (((1,), (1,)), ((), ())), preferred_element_type=F32)


def _ada_body(ct_ref, w_ref, b_ref, o_ref, sb, *, kc):
    d, ng = ct_ref.shape
    tn = w_ref.shape[-1]

    @pl.when(jnp.logical_and(pl.program_id(0) == 0, pl.program_id(1) == 0))
    def _():
        def fill(c, carry):
            k0 = pl.multiple_of(c * kc, kc)
            s = jax.nn.silu(ct_ref[pl.ds(k0, kc), :])
            for gi in range(ng):
                sb[gi, pl.ds(k0, kc), :] = jnp.broadcast_to(s[:, gi:gi + 1], (kc, LANES))
            return carry

        lax.fori_loop(0, d // kc, fill, 0)

    def body(c, accs):
        k0 = pl.multiple_of(c * kc, kc)
        w = w_ref[0, pl.ds(k0, kc), :]
        out = []
        for gi, acc in enumerate(accs):
            s = sb[gi, pl.ds(k0, kc), :]
            parts = [jnp.sum((w[:, t * LANES:(t + 1) * LANES] * s).reshape(kc // 8, 8, LANES), axis=0)
                     for t in range(tn // LANES)]
            out.append(acc + jnp.concatenate(parts, axis=1))
        return tuple(out)

    accs = lax.fori_loop(0, d // kc, body, tuple(jnp.zeros((8, tn), F32) for _ in range(ng)))
    for gi in range(ng):
        o_ref[0, gi:gi + 1, :] = jnp.sum(accs[gi], axis=0, keepdims=True) + b_ref[0]


def _ada(cond_t, w_ada, b_ada):
    d, ng = cond_t.shape
    nl, _, n = w_ada.shape
    tn = 1024
    return pl.pallas_call(
        functools.partial(_ada_body, kc=32),
        grid=(nl, n // tn),
        in_specs=[
            pl.BlockSpec((d, ng), lambda l, j: (0, 0)),
            pl.BlockSpec((1, d, tn), lambda l, j: (l, 0, j)),
            pl.BlockSpec((1, 1, tn), lambda l, j: (l, 0, j)),
        ],
        out_specs=pl.BlockSpec((1, ng, tn), lambda l, j: (l, 0, j)),
        out_shape=jax.ShapeDtypeStruct((nl, ng, n), F32),
        scratch_shapes=[pltpu.VMEM((ng, d, LANES), F32)],
        compiler_params=_cparams(("arbitrary", "arbitrary")),
        name="ada",
    )(cond_t, w_ada, b_ada.reshape(nl, 1, n))


def _inproj_body(xc_ref, xl_ref, mod_ref, g_ref, w_ref, z_ref, h_scr, *, rc, n_ctx_blocks):
    @pl.when(pl.program_id(1) == 0)
    def _():
        shift = mod_ref[0, 0:1, :]
        scale = mod_ref[0, 1:2, :]
        g = g_ref[...]

        def fill(x_ref):
            def body(c, carry):
                r0 = pl.multiple_of(c * rc, rc)
                h = _norm_mod(x_ref[pl.ds(r0, rc), :], g, shift, scale)
                h_scr[pl.ds(r0, rc), :] = h.astype(BF16)
                return carry

            lax.fori_loop(0, x_ref.shape[0] // rc, body, 0)

        is_ctx = pl.program_id(0) < n_ctx_blocks
        pl.when(is_ctx)(lambda: fill(xc_ref))
        pl.when(jnp.logical_not(is_ctx))(lambda: fill(xl_ref))

    z_ref[...] = _dot(h_scr[...], w_ref[...])


def _group_of_block(i, rows, t_ctx, ds):
    r = i * rows
    return jnp.where(r < t_ctx, 0, 1 + (r - t_ctx) // ds)


def _pair_specs(rows, width, n_ctx_blocks):
    return [
        pl.BlockSpec((rows, width), lambda i, *_: (jnp.minimum(i, n_ctx_blocks - 1), 0)),
        pl.BlockSpec((rows, width), lambda i, *_: (jnp.maximum(i - n_ctx_blocks, 0), 0)),
    ]


def _inproj(x_ctx, x_lat, mod, g1, w_in, *, ds):
    t_ctx, d = x_ctx.shape
    t = t_ctx + x_lat.shape[0]
    n = w_in.shape[1]
    tm = next(m for m in TM_PROJ_CHOICES if t_ctx % m == 0 and ds % m == 0)
    tn = 768
    assert n % tn == 0
    grp = functools.partial(_group_of_block, rows=tm, t_ctx=t_ctx, ds=ds)
    return pl.pallas_call(
        functools.partial(_inproj_body, rc=64, n_ctx_blocks=t_ctx // tm),
        grid=(t // tm, n // tn),
        in_specs=_pair_specs(tm, d, t_ctx // tm) + [
            pl.BlockSpec((1, 6, d), lambda i, j: (grp(i), 0, 0)),
            pl.BlockSpec((1, d), lambda i, j: (0, 0)),
            pl.BlockSpec((d, tn), lambda i, j: (0, j)),
        ],
        out_specs=pl.BlockSpec((tm, tn), lambda i, j: (i, j)),
        out_shape=jax.ShapeDtypeStruct((t, n), F32),
        scratch_shapes=[pltpu.VMEM((tm, d), BF16)],
        compiler_params=_cparams(("arbitrary", "arbitrary")),
        name="inproj",
    )(x_ctx, x_lat, mod, g1, w_in)


def _lam_value(lamp_ref, lam_init):
    lp = lamp_ref[...]
    a = jnp.sum(lp[0:1] * lp[1:2], keepdims=True)
    b = jnp.sum(lp[2:3] * lp[3:4], keepdims=True)
    return jnp.exp(a) - jnp.exp(b) + lam_init


def _split_halves(q):
    first = lax.broadcasted_iota(I32, (1, HEAD_W), 1) < QK_DIM
    return jnp.where(first, q, 0.0).astype(BF16), jnp.where(first, 0.0, q).astype(BF16)


def _attn_ctx_body(lamp_ref, lng_ref, q_ref, k_ref, v_ref, o_ref, ko_ref, vo_ref, *, lam_init):
    lam = _lam_value(lamp_ref, lam_init)
    g = lng_ref[...] * (1.0 - lam_init)
    ko_ref[...] = k_ref[...]
    vo_ref[...] = v_ref[...]
    for h in range(A_HEADS):
        hs = slice(h * HEAD_W, (h + 1) * HEAD_W)
        k = k_ref[:, hs].astype(BF16)
        v = v_ref[:, hs].astype(BF16)
        ps = []
        for qh in _split_halves(q_ref[:, hs] * QK_SCALE):
            s = _dot_nt(qh, k)
            e = jnp.exp(s - jnp.max(s, axis=-1, keepdims=True))
            ps.append(e * (1.0 / jnp.sum(e, axis=-1, keepdims=True)))
        o = _dot((ps[0] - lam * ps[1]).astype(BF16), v)
        o_ref[:, hs] = (_rms_lanes(o) * g).astype(o_ref.dtype)


def _attn_ctx(z, lamp, lng, *, nb, s, lam_init):
    return pl.pallas_call(
        functools.partial(_attn_ctx_body, lam_init=lam_init),
        grid=(nb,),
        in_specs=[
            pl.BlockSpec(lamp.shape, lambda b: (0, 0)),
            pl.BlockSpec((1, V_DIM), lambda b: (0, 0)),
            pl.BlockSpec((s, Q_W), lambda b: (b, 0)),
            pl.BlockSpec((s, Q_W), lambda b: (b, 1)),
            pl.BlockSpec((s, A_WIDTH), lambda b: (b, 2)),
        ],
        out_specs=[pl.BlockSpec((s, A_WIDTH), lambda b: (b, 0))] * 3,
        out_shape=[
            jax.ShapeDtypeStruct((nb * s, A_WIDTH), BF16),
            jax.ShapeDtypeStruct((nb * s, Q_W), F32),
            jax.ShapeDtypeStruct((nb * s, A_WIDTH), F32),
        ],
        compiler_params=_cparams(("parallel",)),
        name="attn_ctx",
    )(lamp, lng, z, z, z)


def _rope(x, c, sa, sb):
    return x * c + pltpu.roll(x, HEAD_W - N_FREQ, 1) * sa + pltpu.roll(x, N_FREQ, 1) * sb


def _attn_lat_body(lamp_ref, lng_ref, q_ref, k_ref, v_ref, ck_ref, cv_ref,
                   cq_ref, saq_ref, sbq_ref, ckk_ref, sak_ref, sbk_ref,
                   o_ref, kall, vall, s_scr, e_scr, *, lam_init, kc, rc):
    ds = k_ref.shape[0]
    p = ck_ref.shape[2]
    tq = q_ref.shape[0]
    nch = (ds + p) // kc

    @pl.when(pl.program_id(2) == 0)
    def _():
        def ones_col(rows):
            return jnp.where(lax.broadcasted_iota(I32, (rows, LANES), 1) == 0, 1.0, 0.0).astype(BF16)

        def body(c, carry):
            r0 = pl.multiple_of(c * rc, rc)
            rs = pl.ds(r0, rc)
            kall[rs, :] = _rope(k_ref[rs, :], ckk_ref[rs, :], sak_ref[rs, :], sbk_ref[rs, :]).astype(BF16)
            vall[rs, 0:V_DIM] = v_ref[rs, :].astype(BF16)
            vall[rs, V_DIM:] = ones_col(rc)
            return carry

        lax.fori_loop(0, ds // rc, body, 0)
        kall[ds:ds + p, :] = ck_ref[0, 0].astype(BF16)
        vall[ds:ds + p, 0:V_DIM] = cv_ref[0, 0].astype(BF16)
        vall[ds:ds + p, V_DIM:] = ones_col(p)

    lam = _lam_value(lamp_ref, lam_init)
    q = _rope(q_ref[...], cq_ref[...], saq_ref[...], sbq_ref[...]) * (QK_SCALE * LOG2E)
    halves = _split_halves(q)
    chunks = [slice(c * kc, (c + 1) * kc) for c in range(nch)]

    def qk(n, ch, mrun):
        s = _dot_nt(halves[n], kall[ch, :])
        s_scr[n, :, ch] = s
        for t in range(kc // LANES):
            mrun = jnp.maximum(mrun, s[:, t * LANES:(t + 1) * LANES])
        return mrun

    def ex(n, ch, m):
        e_scr[n, :, ch] = jnp.exp2(s_scr[n, :, ch] - m).astype(BF16)

    neg = jnp.full((tq, LANES), -jnp.inf, F32)
    mrun = neg
    for ch in chunks:
        mrun = qk(0, ch, mrun)
    m0 = jnp.max(mrun, axis=-1, keepdims=True)
    mrun = neg
    for ch in chunks:
        mrun = qk(1, ch, mrun)
        ex(0, ch, m0)
    m1 = jnp.max(mrun, axis=-1, keepdims=True)
    oe0 = jnp.zeros((tq, V_DIM + LANES), F32)
    for ch in chunks:
        ex(1, ch, m1)
        oe0 = oe0 + _dot(e_scr[0, :, ch], vall[ch, :])
    oe1 = _dot(e_scr[1], vall[...])
    outs = [oe[:, 0:V_DIM] * (1.0 / oe[:, V_DIM:V_DIM + 1]) for oe in (oe0, oe1)]
    o = outs[0] - lam * outs[1]
    o_ref[...] = (_rms_lanes(o) * (lng_ref[...] * (1.0 - lam_init))).astype(o_ref.dtype)


def _attn_lat(z, cache_k, cache_v, rope_tabs, lamp, lng, *, layer, t_ctx, nb, ds, lam_init):
    p = cache_k.shape[2]
    tq = min(TQ_ATT, ds)
    sk = ds + p
    kc = 512 if sk % 512 == 0 else LANES
    assert t_ctx % ds == 0 and ds % tq == 0 and sk % kc == 0
    seq0, q0 = t_ctx // ds, t_ctx // tq
    nq = ds // tq
    qh, kh, vh = 0, Q_W // HEAD_W, 2 * Q_W // HEAD_W
    tab_q = pl.BlockSpec((tq, HEAD_W), lambda b, h, qi: (qi, 0))
    tab_k = pl.BlockSpec((ds, HEAD_W), lambda b, h, qi: (0, 0))
    cos, sa, sb = rope_tabs
    return pl.pallas_call(
        functools.partial(_attn_lat_body, lam_init=lam_init, kc=kc, rc=256),
        grid=(nb, A_HEADS, nq),
        in_specs=[
            pl.BlockSpec(lamp.shape, lambda b, h, qi: (0, 0)),
            pl.BlockSpec((1, V_DIM), lambda b, h, qi: (0, 0)),
            pl.BlockSpec((tq, HEAD_W), lambda b, h, qi: (q0 + b * nq + qi, qh + h)),
            pl.BlockSpec((ds, HEAD_W), lambda b, h, qi: (seq0 + b, kh + h)),
            pl.BlockSpec((ds, HEAD_W), lambda b, h, qi: (seq0 + b, vh + h)),
            pl.BlockSpec((1, 1, p, HEAD_W), lambda b, h, qi: (b, layer, 0, h)),
            pl.BlockSpec((1, 1, p, V_DIM), lambda b, h, qi: (b, layer, 0, h)),
            tab_q, tab_q, tab_q, tab_k, tab_k, tab_k,
        ],
        out_specs=pl.BlockSpec((tq, V_DIM), lambda b, h, qi: (b * nq + qi, h)),
        out_shape=jax.ShapeDtypeStruct((nb * ds, A_WIDTH), BF16),
        scratch_shapes=[
            pltpu.VMEM((sk, HEAD_W), BF16),
            pltpu.VMEM((sk, V_DIM + LANES), BF16),
            pltpu.VMEM((2, tq, sk), F32),
            pltpu.VMEM((2, tq, sk), BF16),
        ],
        compiler_params=_cparams(("parallel", "parallel", "arbitrary")),
        name="attn_lat",
    )(lamp, lng, z, z, z, cache_k, cache_v, cos, sa, sb, cos, sa, sb)


def _rope_tables(n_tok):
    n_rows = n_tok // GRID_W
    row = jnp.repeat(jnp.arange(n_rows), GRID_W).astype(F32)
    col = jnp.tile(jnp.arange(GRID_W), n_rows).astype(F32)
    inv = 1.0 / (ROPE_BASE ** (jnp.arange(N_FREQ, dtype=F32) / N_FREQ))
    ang = jnp.stack([row[:, None] * inv, col[:, None] * inv], axis=1)
    cos, sin = jnp.cos(ang), jnp.sin(ang)
    zero = jnp.zeros_like(sin)

    def lanes(first, second):
        per_axis = jnp.concatenate([first, second], axis=-1)
        return jnp.tile(per_axis.reshape(n_tok, 2 * 2 * N_FREQ), (1, 2))

    return lanes(cos, cos), lanes(-sin, zero), lanes(zero, sin)


def _pack_pair(hi, lo):
    hb = lax.bitcast_convert_type(hi.astype(BF16).astype(F32), U32)
    lb = lax.bitcast_convert_type(lo.astype(BF16).astype(F32), U32)
    return hb | (lb >> 16)


def _unpack_pair(u):
    hi = lax.bitcast_convert_type(u & jnp.uint32(0xFFFF0000), F32)
    lo = lax.bitcast_convert_type(u << 16, F32)
    return hi.astype(BF16), lo.astype(BF16)


def _hi_lo(x):
    hi = x.astype(BF16)
    return hi, (x - hi.astype(F32)).astype(BF16)


def _mixpost_body(oac_ref, oal_ref, zb_ref, zc_ref, zp_ref, zn_ref, xc_ref, xl_ref, mod_ref, g2_ref,
                  wout_ref, gng_ref, ws_ref, gb_ref, pw_ref, ps_ref, band_ref, bandp_ref, bandn_ref,
                  rw_ref, rb_ref, xo_ref, h_ref, lg_ref, cat_scr, *, n_ctx_tiles, tps_ctx, tps_lat):
    i = pl.program_id(0)
    is_ctx = i < n_ctx_tiles
    pos = jnp.where(is_ctx, i % tps_ctx, (i - n_ctx_tiles) % tps_lat)
    first = pos == 0
    last = pos == jnp.where(is_ctx, tps_ctx, tps_lat) - 1

    cat_scr[:, 0:A_WIDTH] = jnp.where(is_ctx, oac_ref[...], oal_ref[...])

    zb = jax.nn.gelu(zb_ref[...])
    u, v = zb[:, :B_WIDTH], zb[:, B_WIDTH:]
    for g in range(B_GROUPS):
        gs = slice(g * B_CH, (g + 1) * B_CH)
        vn = (_rms_lanes(v[:, gs]) * gng_ref[:, gs]).astype(BF16)
        for n in range(TILE // CHUNK):
            rs = slice(n * CHUNK, (n + 1) * CHUNK)
            mixed = _dot(ws_ref[g], vn[rs]) + gb_ref[:, gs]
            cat_scr[rs, A_WIDTH + g * B_CH:A_WIDTH + (g + 1) * B_CH] = (u[rs, gs] * mixed).astype(BF16)

    xc = zc_ref[...]
    xc_parts = _hi_lo(xc)
    xp_parts = _hi_lo(zp_ref[...] * jnp.where(first, 0.0, 1.0))
    xn_parts = _hi_lo(zn_ref[...] * jnp.where(last, 0.0, 1.0))
    r = lax.broadcasted_iota(I32, (TILE, 1), 0)
    for g, w in enumerate(POOL_WINDOWS):
        gs = slice(g * C_CH, (g + 1) * C_CH)
        half = w // 2
        acc = jnp.zeros((TILE, C_CH), F32)
        for part in xc_parts:
            acc = acc + _dot(band_ref[g], part[:, gs])
        for part in xp_parts:
            acc = acc + _dot(bandp_ref[g], part[:, gs])
        for part in xn_parts:
            acc = acc + _dot(bandn_ref[g], part[:, gs])
        left = jnp.where(first, jnp.minimum(half, r), half)
        right = jnp.where(last, jnp.minimum(half - 1, TILE - 1 - r), half - 1)
        cnt = (left + right + 1).astype(F32)
        pooled = acc / cnt - xc[:, gs]
        y = _dot(pooled.astype(BF16), pw_ref[g]) * ps_ref[:, gs]
        c0 = A_WIDTH + B_WIDTH + g * C_CH
        cat_scr[:, c0:c0 + C_CH] = y.astype(BF16)

    mix = _dot(cat_scr[...], wout_ref[...])
    x_new = jnp.where(is_ctx, xc_ref[...], xl_ref[...]) + mod_ref[0, 2:3, :] * mix
    xo_ref[...] = x_new
    h2 = _norm_mod(x_new, g2_ref[...], mod_ref[0, 3:4, :], mod_ref[0, 4:5, :])
    half_d = h2.shape[1] // 2
    h_ref[...] = _pack_pair(h2[:, :half_d], h2[:, half_d:])
    h_hi, h_lo = _hi_lo(h2)
    lg_ref[...] = _dot(h_hi, rw_ref[0]) + (_dot(h_lo, rw_ref[0]) + _dot(h_hi, rw_ref[1])) + rb_ref[...]


def _band_matrices():
    r = np.arange(TILE)[:, None]
    cur, prev, nxt = [], [], []
    for w in POOL_WINDOWS:
        half = w // 2
        c = np.arange(TILE)[None, :]
        cur.append((c >= r - half) & (c <= r + half - 1))
        ch = np.arange(HALO)[None, :]
        prev.append(ch - HALO >= r - half)
        nxt.append(TILE + ch <= r + half - 1)
    to = lambda m: jnp.asarray(np.stack(m).astype(np.float32), dtype=BF16)
    return to(cur), to(prev), to(nxt)


def _mixpost(oa_ctx, oa_lat, z, x_ctx, x_lat, mod, g2, w_out, gng, ws, gb, pw, ps, rw, rb, *, s_ctx, ds):
    t_ctx, d = x_ctx.shape
    t = t_ctx + x_lat.shape[0]
    nt = t // TILE
    n_ctx_tiles = t_ctx // TILE
    tps_ctx, tps_lat = s_ctx // TILE, ds // TILE
    assert s_ctx % TILE == 0 and ds % TILE == 0
    grp = functools.partial(_group_of_block, rows=TILE, t_ctx=t_ctx, ds=ds)
    band, bandp, bandn = _band_matrices()
    zb_blk = (2 * Q_W + A_WIDTH) // (2 * B_WIDTH)
    zc_blk = (2 * Q_W + A_WIDTH + 2 * B_WIDTH) // C_WIDTH
    hpt = TILE // HALO
    const2 = lambda i: (0, 0)
    const3 = lambda i: (0, 0, 0)
    return pl.pallas_call(
        functools.partial(_mixpost_body, n_ctx_tiles=n_ctx_tiles, tps_ctx=tps_ctx, tps_lat=tps_lat),
        grid=(nt,),
        in_specs=_pair_specs(TILE, A_WIDTH, n_ctx_tiles) + [
            pl.BlockSpec((TILE, 2 * B_WIDTH), lambda i: (i, zb_blk)),
            pl.BlockSpec((TILE, C_WIDTH), lambda i: (i, zc_blk)),
            pl.BlockSpec((HALO, C_WIDTH), lambda i: (jnp.maximum(i * hpt - 1, 0), zc_blk)),
            pl.BlockSpec((HALO, C_WIDTH), lambda i: (jnp.minimum((i + 1) * hpt, nt * hpt - 1), zc_blk)),
        ] + _pair_specs(TILE, d, n_ctx_tiles) + [
            pl.BlockSpec((1, 6, d), lambda i: (grp(i), 0, 0)),
            pl.BlockSpec((1, d), const2),
            pl.BlockSpec(w_out.shape, const2),
            pl.BlockSpec((1, B_WIDTH), const2),
            pl.BlockSpec(ws.shape, const3),
            pl.BlockSpec(gb.shape, const2),
            pl.BlockSpec(pw.shape, const3),
            pl.BlockSpec((1, C_WIDTH), const2),
            pl.BlockSpec(band.shape, const3),
            pl.BlockSpec(bandp.shape, const3),
            pl.BlockSpec(bandn.shape, const3),
            pl.BlockSpec(rw.shape, const3),
            pl.BlockSpec(rb.shape, const2),
        ],
        out_specs=[
            pl.BlockSpec((TILE, d), lambda i: (i, 0)),
            pl.BlockSpec((TILE, d // 2), lambda i: (i, 0)),
            pl.BlockSpec((TILE, LANES), lambda i: (i, 0)),
        ],
        out_shape=[
            jax.ShapeDtypeStruct((t, d), F32),
            jax.ShapeDtypeStruct((t, d // 2), U32),
            jax.ShapeDtypeStruct((t, LANES), F32),
        ],
        scratch_shapes=[pltpu.VMEM((TILE, d), BF16)],
        compiler_params=_cparams(("arbitrary",)),
        name="mixpost",
    )(oa_ctx, oa_lat, z, z, z, z, x_ctx, x_lat, mod, g2, w_out, gng, ws, gb, pw, ps, band, bandp, bandn, rw, rb)


def _route_body(lg_ref, tri_ref, idx_ref, gate_ref, rank_ref, cnt_ref, carry, *, n_exp):
    @pl.when(pl.program_id(0) == 0)
    def _():
        carry[...] = jnp.zeros_like(carry)

    lane = lax.broadcasted_iota(I32, lg_ref.shape, 1).astype(F32)
    l = jnp.where(lane < n_exp, lg_ref[...], -jnp.inf)
    vals, idxs, sels = [], [], []
    for _ in range(TOP_K):
        m = jnp.max(l, axis=-1, keepdims=True)
        idx = jnp.min(jnp.where(l == m, lane, float(LANES)), axis=-1, keepdims=True)
        sel = lane == idx
        vals.append(m)
        idxs.append(idx)
        sels.append(sel)
        l = jnp.where(sel, -jnp.inf, l)
    chosen = functools.reduce(jnp.logical_or, sels)
    chosen_f = jnp.where(chosen, 1.0, 0.0)
    prefix = carry[...] + _dot(tri_ref[...], chosen_f.astype(BF16))
    es = [jnp.exp(v - vals[0]) for v in vals]
    tot = functools.reduce(jnp.add, es)
    idx_o = jnp.zeros(lg_ref.shape, F32)
    gate_o = jnp.zeros(lg_ref.shape, F32)
    rank_o = jnp.zeros(lg_ref.shape, F32)
    for k in range(TOP_K):
        rank_k = jnp.sum(jnp.where(sels[k], prefix, 0.0), axis=-1, keepdims=True)
        idx_o = jnp.where(lane == k, idxs[k], idx_o)
        gate_o = jnp.where(lane == k, es[k] / tot, gate_o)
        rank_o = jnp.where(lane == k, rank_k, rank_o)
    idx_ref[...] = idx_o.astype(I32)
    gate_ref[...] = gate_o
    rank_ref[...] = rank_o.astype(I32)
    carry[...] = carry[...] + jnp.sum(chosen_f, axis=0, keepdims=True)
    cnt_ref[...] = carry[...]


def _route(logits, n_exp):
    t = logits.shape[0]
    tri = jnp.asarray(np.tril(np.ones((TILE, TILE), np.float32), -1), dtype=BF16)
    blk = pl.BlockSpec((TILE, LANES), lambda i: (i, 0))
    return pl.pallas_call(
        functools.partial(_route_body, n_exp=n_exp),
        grid=(t // TILE,),
        in_specs=[blk, pl.BlockSpec((TILE, TILE), lambda i: (0, 0))],
        out_specs=[blk, blk, blk, pl.BlockSpec((1, LANES), lambda i: (0, 0))],
        out_shape=[
            jax.ShapeDtypeStruct((t, LANES), I32),
            jax.ShapeDtypeStruct((t, LANES), F32),
            jax.ShapeDtypeStruct((t, LANES), I32),
            jax.ShapeDtypeStruct((1, LANES), F32),
        ],
        scratch_shapes=[pltpu.VMEM((1, LANES), F32)],
        compiler_params=_cparams(("arbitrary",)),
        name="route",
    )(logits, tri)


DMA_UNROLL = 8


def _dispatch_body(fill_ref, dest_ref, h_ref, xb_ref, zbuf, sem, zsem, *, n_exp):
    rows = h_ref.shape[0]

    @pl.when(pl.program_id(0) == 0)
    def _():
        zbuf[...] = jnp.zeros_like(zbuf)

        def fill_copy(e):
            start = pl.multiple_of(fill_ref[0, e], SUB_MOE)
            return pltpu.make_async_copy(zbuf, xb_ref.at[pl.ds(start, SUB_MOE)], zsem)

        for e in range(n_exp):
            pl.when(fill_ref[1, e] > 0)(lambda e=e: fill_copy(e).start())
        for e in range(n_exp):
            pl.when(fill_ref[1, e] > 0)(lambda e=e: fill_copy(e).wait())

    def row_copy(r, d):
        return pltpu.make_async_copy(h_ref.at[pl.ds(r, 1)], xb_ref.at[pl.ds(d, 1)], sem)

    def for_rows(fn):
        def body(g, carry):
            r0 = pl.multiple_of(g * DMA_UNROLL, DMA_UNROLL)
            for u in range(DMA_UNROLL):
                for k in range(TOP_K):
                    fn(row_copy(r0 + u, dest_ref[0, 0, (r0 + u) * TOP_K + k]))
            return carry

        lax.fori_loop(0, rows // DMA_UNROLL, body, 0)

    for_rows(lambda cp: cp.start())
    for_rows(lambda cp: cp.wait())


def _dispatch(fill, dest_tiles, h_packed, n_slots):
    t, w = h_packed.shape
    return pl.pallas_call(
        functools.partial(_dispatch_body, n_exp=fill.shape[1]),
        grid=(t // TILE,),
        in_specs=[
            pl.BlockSpec(memory_space=pltpu.SMEM),
            pl.BlockSpec((1, 1, TILE * TOP_K), lambda i: (i, 0, 0), memory_space=pltpu.SMEM),
            pl.BlockSpec((TILE, w), lambda i: (i, 0)),
        ],
        out_specs=pl.BlockSpec(memory_space=pl.ANY),
        out_shape=jax.ShapeDtypeStruct((n_slots, w), U32),
        scratch_shapes=[pltpu.VMEM((SUB_MOE, w), U32), pltpu.SemaphoreType.DMA(()), pltpu.SemaphoreType.DMA(())],
        compiler_params=_cparams(("arbitrary",)),
        name="dispatch",
    )(fill, dest_tiles, h_packed)


def _moe_body(be_ref, bv_ref, nu_ref, xb_ref, wg_ref, wu_ref, wd_ref, bg_ref, bu_ref, bd_ref,
              y_ref, x_scr):
    del be_ref, nu_ref
    i, j = pl.program_id(0), pl.program_id(1)
    valid = bv_ref[i]
    tm = xb_ref.shape[0]
    half_d = xb_ref.shape[1]
    n_sub = (valid + SUB_MOE - 1) // SUB_MOE

    @pl.when(valid > 0)
    def _():
        @pl.when(j == 0)
        def _():
            def unpack(s, carry):
                rs = pl.ds(pl.multiple_of(s * SUB_MOE, SUB_MOE), SUB_MOE)
                hi, lo = _unpack_pair(xb_ref[rs, :])
                x_scr[rs, 0:half_d] = hi
                x_scr[rs, half_d:] = lo
                return carry

            def init(s, carry):
                rs = pl.ds(pl.multiple_of(s * SUB_MOE, SUB_MOE), SUB_MOE)
                y_ref[rs, :] = jnp.broadcast_to(bd_ref[0, 0], (SUB_MOE, y_ref.shape[1]))
                return carry

            lax.fori_loop(0, n_sub, unpack, 0)
            lax.fori_loop(0, tm // SUB_MOE, init, 0)

        def ffn(rows):
            x = x_scr[0:rows, :]
            g = _dot(x, wg_ref[0, 0].astype(BF16)) + bg_ref[0, 0]
            u = _dot(x, wu_ref[0, 0].astype(BF16)) + bu_ref[0, 0]
            g = jnp.minimum(g, SWIGLU_LIMIT)
            u = jnp.clip(u, -SWIGLU_LIMIT, SWIGLU_LIMIT)
            a = g * jax.nn.sigmoid(SWIGLU_ALPHA * g) * (u + 1.0)
            y_ref[0:rows, :] += _dot(a.astype(BF16), wd_ref[0, 0].astype(BF16))

        for k in range(1, tm // SUB_MOE + 1):
            pl.when(n_sub == k)(functools.partial(ffn, k * SUB_MOE))


def _moe(blk_e, blk_valid, n_used, xb, w_gu, b_gu, w_dn, b_dn, *, layer):
    n_slots, half_d = xb.shape
    d = 2 * half_d
    n_exp, _, two_f = w_gu.shape[1:]
    f = two_f // 2
    tm, tf = TM_MOE, TF_MOE
    nf = f // tf
    nb = n_slots // tm

    def live(i, nu):
        return i < nu[0]

    def row_blk(i, j, be, bv, nu):
        return (jnp.minimum(i, nu[0] - 1), 0)

    def jj(i, j, nu):
        return jnp.where(live(i, nu), j, nf - 1)

    grid_spec = pltpu.PrefetchScalarGridSpec(
        num_scalar_prefetch=3,
        grid=(nb, nf),
        in_specs=[
            pl.BlockSpec((tm, half_d), row_blk),
            pl.BlockSpec((1, 1, d, tf), lambda i, j, be, bv, nu: (layer, be[i], 0, jj(i, j, nu))),
            pl.BlockSpec((1, 1, d, tf), lambda i, j, be, bv, nu: (layer, be[i], 0, nf + jj(i, j, nu))),
            pl.BlockSpec((1, 1, tf, d), lambda i, j, be, bv, nu: (layer, be[i], jj(i, j, nu), 0)),
            pl.BlockSpec((1, 1, 1, tf), lambda i, j, be, bv, nu: (layer, be[i], 0, jj(i, j, nu))),
            pl.BlockSpec((1, 1, 1, tf), lambda i, j, be, bv, nu: (layer, be[i], 0, nf + jj(i, j, nu))),
            pl.BlockSpec((1, 1, 1, d), lambda i, j, be, bv, nu: (layer, be[i], 0, 0)),
        ],
        out_specs=pl.BlockSpec((tm, d), row_blk),
        scratch_shapes=[pltpu.VMEM((tm, d), BF16)],
    )
    nl = w_gu.shape[0]
    return pl.pallas_call(
        _moe_body,
        grid_spec=grid_spec,
        out_shape=jax.ShapeDtypeStruct((n_slots, d), F32),
        compiler_params=_cparams(("arbitrary", "arbitrary")),
        name="moe",
    )(blk_e, blk_valid, n_used, xb, w_gu, w_gu, w_dn,
      b_gu.reshape(nl, n_exp, 1, two_f), b_gu.reshape(nl, n_exp, 1, two_f), b_dn.reshape(nl, n_exp, 1, d))


def _combine_body(dest_ref, destn_ref, gate_ref, x_ref, mod_ref, fg_ref, yb_ref, oc_ref, ol_ref, buf, sems,
                  *, final, n_ctx_tiles):
    rows = x_ref.shape[0]
    i = pl.program_id(0)
    cur = i % 2

    def row_copy(b, r, k, d):
        return pltpu.make_async_copy(yb_ref.at[pl.ds(d, 1)], buf.at[b, k, pl.ds(r, 1)], sems.at[b])

    def for_rows(b, d_ref, fn):
        def body(g, carry):
            r0 = pl.multiple_of(g * DMA_UNROLL, DMA_UNROLL)
            for u in range(DMA_UNROLL):
                for k in range(TOP_K):
                    fn(row_copy(b, r0 + u, k, d_ref[0, 0, (r0 + u) * TOP_K + k]))
            return carry

        lax.fori_loop(0, rows // DMA_UNROLL, body, 0)

    @pl.when(i == 0)
    def _():
        for_rows(0, dest_ref, lambda cp: cp.start())

    @pl.when(i + 1 < pl.num_programs(0))
    def _():
        for_rows(1 - cur, destn_ref, lambda cp: cp.start())

    for_rows(cur, dest_ref, lambda cp: cp.wait())

    ff = gate_ref[:, 0:1] * buf[cur, 0]
    for k in range(1, TOP_K):
        ff = ff + gate_ref[:, k:k + 1] * buf[cur, k]
    x_new = x_ref[...] + mod_ref[0, 5:6, :] * ff
    if final:
        x_new = _rms_lanes(x_new) * fg_ref[...]
    is_ctx = pl.program_id(0) < n_ctx_tiles

    @pl.when(is_ctx)
    def _():
        oc_ref[...] = x_new

    @pl.when(jnp.logical_not(is_ctx))
    def _():
        ol_ref[...] = x_new


def _combine(dest_tiles, gates, x, mod, final_g, yb, *, t_ctx, ds, final):
    t, d = x.shape
    n_ctx_tiles = t_ctx // TILE
    grp = functools.partial(_group_of_block, rows=TILE, t_ctx=t_ctx, ds=ds)
    nt = t // TILE
    return pl.pallas_call(
        functools.partial(_combine_body, final=final, n_ctx_tiles=n_ctx_tiles),
        grid=(nt,),
        in_specs=[
            pl.BlockSpec((1, 1, TILE * TOP_K), lambda i: (i, 0, 0), memory_space=pltpu.SMEM),
            pl.BlockSpec((1, 1, TILE * TOP_K), lambda i: (jnp.minimum(i + 1, nt - 1), 0, 0),
                         memory_space=pltpu.SMEM),
            pl.BlockSpec((TILE, LANES), lambda i: (i, 0)),
            pl.BlockSpec((TILE, d), lambda i: (i, 0)),
            pl.BlockSpec((1, 6, d), lambda i: (grp(i), 0, 0)),
            pl.BlockSpec((1, d), lambda i: (0, 0)),
            pl.BlockSpec(memory_space=pl.ANY),
        ],
        out_specs=_pair_specs(TILE, d, n_ctx_tiles),
        out_shape=[jax.ShapeDtypeStruct((t_ctx, d), F32), jax.ShapeDtypeStruct((t - t_ctx, d), F32)],
        scratch_shapes=[pltpu.VMEM((2, TOP_K, TILE, d), F32), pltpu.SemaphoreType.DMA((2,))],
        compiler_params=_cparams(("arbitrary",)),
        name="combine",
    )(dest_tiles, dest_tiles, gates, x, mod, final_g, yb)


def _slot_layout(idx, rank, counts, n_blocks):
    n_exp = counts.shape[0]
    n_blk = (counts + TM_MOE - 1) // TM_MOE
    padded = n_blk * TM_MOE
    pend = jnp.cumsum(padded)
    pstart = pend - padded
    per = jnp.maximum(-(-counts // jnp.maximum(n_blk, 1)), 1)
    per = (per + SUB_MOE - 1) // SUB_MOE * SUB_MOE
    per_tok = per[idx]
    blk_in = jnp.floor((rank.astype(F32) + 0.5) / per_tok.astype(F32)).astype(I32)
    dest = pstart[idx] + blk_in * TM_MOE + (rank - blk_in * per_tok)
    bstart = jnp.arange(n_blocks, dtype=I32) * TM_MOE
    n_used = (pend[-1] // TM_MOE).astype(I32).reshape(1)
    owner_of = jnp.minimum(bstart, jnp.maximum(pend[-1] - TM_MOE, 0))
    blk_e = jnp.minimum(jnp.searchsorted(pend, owner_of, side='right'), n_exp - 1).astype(I32)
    local = (bstart - pstart[blk_e]) // TM_MOE
    blk_valid = jnp.clip(counts[blk_e] - local * per[blk_e], 0, per[blk_e])
    blk_valid = jnp.where(bstart < pend[-1], blk_valid, 0).astype(I32)
    last_blk = jnp.maximum(counts - 1, 0) // per
    last_cnt = counts - last_blk * per
    last_group = pstart + last_blk * TM_MOE + jnp.maximum((last_cnt + SUB_MOE - 1) // SUB_MOE - 1, 0) * SUB_MOE
    fill = jnp.stack([last_group, (counts > 0).astype(I32)]).astype(I32)
    return dest.astype(I32), fill, blk_e, blk_valid, n_used


def kernel(x_prompt, x_sample, cache_k, cache_v, c, c_ctx, norm1_g, norm2_g, w_ada, b_ada, w_in, w_out,
           lam_q1, lam_k1, lam_q2, lam_k2, diff_ln_g, gmlp_norm_g, gmlp_ws, gmlp_bs, pool_w, pool_scale,
           router_w, router_b, exp_w_gu, exp_b_gu, exp_w_down, exp_b_down, final_g):
    nb_ctx, s_ctx, d = x_prompt.shape
    nb_lat, ds, _ = x_sample.shape
    depth = w_in.shape[0]
    n_exp = router_w.shape[-1]
    p_len = cache_k.shape[2]
    t_ctx, t_lat = nb_ctx * s_ctx, nb_lat * ds
    t = t_ctx + t_lat
    assert t_ctx % TILE == 0 and t_lat % TILE == 0 and (t * TOP_K) % TM_MOE == 0

    x_ctx, x_lat = x_prompt.reshape(t_ctx, d), x_sample.reshape(t_lat, d)
    cond_t = jnp.concatenate([c_ctx[None, :], c], axis=0).T
    mod = _ada(cond_t, w_ada, b_ada)
    mod = mod.reshape(depth, 1 + nb_lat, 6, d)

    rope_tabs = _rope_tables(ds)
    ck = cache_k.reshape(nb_lat, depth, p_len, A_HEADS * HEAD_W)
    cv = cache_v.reshape(nb_lat, depth, p_len, A_WIDTH)
    w_in_b = w_in.astype(BF16)
    w_out_b = w_out.astype(BF16)
    ws_b = gmlp_ws.astype(BF16)
    pw_b = pool_w.astype(BF16)
    rw_pad = jnp.pad(router_w, ((0, 0), (0, 0), (0, LANES - n_exp)))
    rw_hi = rw_pad.astype(BF16)
    rw_parts = jnp.stack([rw_hi, (rw_pad - rw_hi.astype(F32)).astype(BF16)], axis=1)
    rb_pad = jnp.pad(router_b, ((0, 0), (0, LANES - n_exp)))
    n_blocks = t * TOP_K // TM_MOE + n_exp

    new_k, new_v = [], []
    for l in range(depth):
        lam_init = 0.8 - 0.6 * math.exp(-0.3 * l)
        lamp = jnp.stack([lam_q1[l], lam_k1[l], lam_q2[l], lam_k2[l]])
        lng = diff_ln_g[l][None, :]
        z = _inproj(x_ctx, x_lat, mod[l], norm1_g[l][None, :], w_in_b[l], ds=ds)
        oa_ctx, k_new, v_new = _attn_ctx(z, lamp, lng, nb=nb_ctx, s=s_ctx, lam_init=lam_init)
        new_k.append(k_new.reshape(nb_ctx, s_ctx, A_HEADS, HEAD_W))
        new_v.append(v_new.reshape(nb_ctx, s_ctx, A_HEADS, V_DIM))
        oa_lat = _attn_lat(z, ck, cv, rope_tabs, lamp, lng, layer=l, t_ctx=t_ctx, nb=nb_lat, ds=ds,
                           lam_init=lam_init)
        gb = jnp.repeat(gmlp_bs[l].T, B_CH, axis=1)
        x_mid, h_packed, logits = _mixpost(
            oa_ctx, oa_lat, z, x_ctx, x_lat, mod[l], norm2_g[l][None, :], w_out_b[l], gmlp_norm_g[l][None, :],
            ws_b[l], gb, pw_b[l], pool_scale[l][None, :], rw_parts[l], rb_pad[l][None, :], s_ctx=s_ctx, ds=ds)
        idx, gates, rank, counts = _route(logits, n_exp)
        dest, fill, blk_e, blk_valid, n_used = _slot_layout(
            idx[:, :TOP_K], rank[:, :TOP_K], counts[0, :n_exp].astype(I32), n_blocks)
        dest_tiles = dest.reshape(t // TILE, 1, TILE * TOP_K)
        xb = _dispatch(fill, dest_tiles, h_packed, n_blocks * TM_MOE)
        yb = _moe(blk_e, blk_valid, n_used, xb, exp_w_gu, exp_b_gu, exp_w_down, exp_b_down, layer=l)
        x_ctx, x_lat = _combine(dest_tiles, gates, x_mid, mod[l], final_g[None, :], yb, t_ctx=t_ctx, ds=ds,
                                final=(l == depth - 1))

    return (x_ctx.reshape(nb_ctx, s_ctx, d), x_lat.reshape(nb_lat, ds, d),
            jnp.stack(new_k, axis=1), jnp.stack(new_v, axis=1))
```

```python
import functools
import math

import numpy as np
import jax
import jax.numpy as jnp
from jax import lax
from jax.experimental import pallas as pl
from jax.experimental.pallas import tpu as pltpu

F32 = jnp.float32
BF16 = jnp.bfloat16
I32 = jnp.int32
U32 = jnp.uint32

GRID_W = 64
A_HEADS = 8
QK_DIM = 64
V_DIM = 2 * QK_DIM
HEAD_W = 2 * QK_DIM
Q_W = A_HEADS * HEAD_W
A_WIDTH = A_HEADS * V_DIM
B_GROUPS = 4
CHUNK = 128
B_CH = 128
B_WIDTH = B_GROUPS * B_CH
POOL_WINDOWS = (2, 4, 8, 16)
N_POOL = len(POOL_WINDOWS)
C_CH = 128
C_WIDTH = N_POOL * C_CH
ROPE_BASE = 10000.0
N_FREQ = QK_DIM // 4
TOP_K = 4
SWIGLU_ALPHA = 1.702
SWIGLU_LIMIT = 7.0
EPS = 1e-6
QK_SCALE = QK_DIM ** -0.5

LANES = 128
TILE = 256
HALO = 16
TM_PROJ_CHOICES = (1024, 512, 256)
LOG2E = 1.4426950408889634
TM_MOE = 1024
SUB_MOE = 128
TF_MOE = 256
TQ_ATT = 512
VMEM_LIMIT = 56 * 1024 * 1024


def _cparams(sem, vmem=VMEM_LIMIT):
    return pltpu.CompilerParams(dimension_semantics=sem, vmem_limit_bytes=vmem)


def _norm_mod(x, g, shift, scale):
    ms = jnp.mean(x * x, axis=-1, keepdims=True)
    return x * lax.rsqrt(ms + EPS) * g * (1.0 + scale) + shift


def _rms_lanes(x):
    return x * lax.rsqrt(jnp.mean(x * x, axis=-1, keepdims=True) + EPS)


def _dot(a, b):
    return jnp.dot(a, b, preferred_element_type=F32)


def _dot_nt(a, b):
    return lax.dot_general(a, b, (((1,), (1,)), ((), ())), preferred_element_type=F32)


def _ada_body(ct_ref, w_ref, b_ref, o_ref, sb, *, kc):
    d, ng = ct_ref.shape
    tn = w_ref.shape[-1]

    @pl.when(jnp.logical_and(pl.program_id(0) == 0, pl.program_id(1) == 0))
    def _():
        def fill(c, carry):
            k0 = pl.multiple_of(c * kc, kc)
            s = jax.nn.silu(ct_ref[pl.ds(k0, kc), :])
            for gi in range(ng):
                sb[gi, pl.ds(k0, kc), :] = jnp.broadcast_to(s[:, gi:gi + 1], (kc, LANES))
            return carry

        lax.fori_loop(0, d // kc, fill, 0)

    def body(c, accs):
        k0 = pl.multiple_of(c * kc, kc)
        w = w_ref[0, pl.ds(k0, kc), :]
        out = []
        for gi, acc in enumerate(accs):
            s = sb[gi, pl.ds(k0, kc), :]
            parts = [jnp.sum((w[:, t * LANES:(t + 1) * LANES] * s).reshape(kc // 8, 8, LANES), axis=0)
                     for t in range(tn // LANES)]
            out.append(acc + jnp.concatenate(parts, axis=1))
        return tuple(out)

    accs = lax.fori_loop(0, d // kc, body, tuple(jnp.zeros((8, tn), F32) for _ in range(ng)))
    for gi in range(ng):
        o_ref[0, gi:gi + 1, :] = jnp.sum(accs[gi], axis=0, keepdims=True) + b_ref[0]


def _ada(cond_t, w_ada, b_ada):
    d, ng = cond_t.shape
    nl, _, n = w_ada.shape
    tn = 1024
    return pl.pallas_call(
        functools.partial(_ada_body, kc=32),
        grid=(nl, n // tn),
        in_specs=[
            pl.BlockSpec((d, ng), lambda l, j: (0, 0)),
            pl.BlockSpec((1, d, tn), lambda l, j: (l, 0, j)),
            pl.BlockSpec((1, 1, tn), lambda l, j: (l, 0, j)),
        ],
        out_specs=pl.BlockSpec((1, ng, tn), lambda l, j: (l, 0, j)),
        out_shape=jax.ShapeDtypeStruct((nl, ng, n), F32),
        scratch_shapes=[pltpu.VMEM((ng, d, LANES), F32)],
        compiler_params=_cparams(("arbitrary", "arbitrary")),
        name="ada",
    )(cond_t, w_ada, b_ada.reshape(nl, 1, n))


def _inproj_body(xc_ref, xl_ref, mod_ref, g_ref, w_ref, z_ref, h_scr, *, rc, n_ctx_blocks):
    @pl.when(pl.program_id(1) == 0)
    def _():
        shift = mod_ref[0, 0:1, :]
        scale = mod_ref[0, 1:2, :]
        g = g_ref[...]

        def fill(x_ref):
            def body(c, carry):
                r0 = pl.multiple_of(c * rc, rc)
                h = _norm_mod(x_ref[pl.ds(r0, rc), :], g, shift, scale)
                h_scr[pl.ds(r0, rc), :] = h.astype(BF16)
                return carry

            lax.fori_loop(0, x_ref.shape[0] // rc, body, 0)

        is_ctx = pl.program_id(0) < n_ctx_blocks
        pl.when(is_ctx)(lambda: fill(xc_ref))
        pl.when(jnp.logical_not(is_ctx))(lambda: fill(xl_ref))

    z_ref[...] = _dot(h_scr[...], w_ref[...])


def _group_of_block(i, rows, t_ctx, ds):
    r = i * rows
    return jnp.where(r < t_ctx, 0, 1 + (r - t_ctx) // ds)


def _pair_specs(rows, width, n_ctx_blocks):
    return [
        pl.BlockSpec((rows, width), lambda i, *_: (jnp.minimum(i, n_ctx_blocks - 1), 0)),
        pl.BlockSpec((rows, width), lambda i, *_: (jnp.maximum(i - n_ctx_blocks, 0), 0)),
    ]


def _inproj(x_ctx, x_lat, mod, g1, w_in, *, ds):
    t_ctx, d = x_ctx.shape
    t = t_ctx + x_lat.shape[0]
    n = w_in.shape[1]
    tm = next(m for m in TM_PROJ_CHOICES if t_ctx % m == 0 and ds % m == 0)
    tn = 768
    assert n % tn == 0
    grp = functools.partial(_group_of_block, rows=tm, t_ctx=t_ctx, ds=ds)
    return pl.pallas_call(
        functools.partial(_inproj_body, rc=64, n_ctx_blocks=t_ctx // tm),
        grid=(t // tm, n // tn),
        in_specs=_pair_specs(tm, d, t_ctx // tm) + [
            pl.BlockSpec((1, 6, d), lambda i, j: (grp(i), 0, 0)),
            pl.BlockSpec((1, d), lambda i, j: (0, 0)),
            pl.BlockSpec((d, tn), lambda i, j: (0, j)),
        ],
        out_specs=pl.BlockSpec((tm, tn), lambda i, j: (i, j)),
        out_shape=jax.ShapeDtypeStruct((t, n), F32),
        scratch_shapes=[pltpu.VMEM((tm, d), BF16)],
        compiler_params=_cparams(("arbitrary", "arbitrary")),
        name="inproj",
    )(x_ctx, x_lat, mod, g1, w_in)


def _lam_value(lamp_ref, lam_init):
    lp = lamp_ref[...]
    a = jnp.sum(lp[0:1] * lp[1:2], keepdims=True)
    b = jnp.sum(lp[2:3] * lp[3:4], keepdims=True)
    return jnp.exp(a) - jnp.exp(b) + lam_init


def _split_halves(q):
    first = lax.broadcasted_iota(I32, (1, HEAD_W), 1) < QK_DIM
    return jnp.where(first, q, 0.0).astype(BF16), jnp.where(first, 0.0, q).astype(BF16)


def _attn_ctx_body(lamp_ref, lng_ref, q_ref, k_ref, v_ref, *rest, lam_init):
    o_ref, ko_ref, vo_ref = rest[-3:]
    lam = _lam_value(lamp_ref, lam_init)
    g = lng_ref[...] * (1.0 - lam_init)
    ko_ref[0, 0] = k_ref[...]
    vo_ref[0, 0] = v_ref[...]
    for h in range(A_HEADS):
        hs = slice(h * HEAD_W, (h + 1) * HEAD_W)
        k = k_ref[:, hs].astype(BF16)
        v = v_ref[:, hs].astype(BF16)
        ps = []
        for qh in _split_halves(q_ref[:, hs] * QK_SCALE):
            s = _dot_nt(qh, k)
            e = jnp.exp(s - jnp.max(s, axis=-1, keepdims=True))
            ps.append(e * (1.0 / jnp.sum(e, axis=-1, keepdims=True)))
        o = _dot((ps[0] - lam * ps[1]).astype(BF16), v)
        o_ref[:, hs] = (_rms_lanes(o) * g).astype(o_ref.dtype)


def _attn_ctx(z, lamp, lng, caches, *, layer, depth, nb, s, lam_init):
    cache_blk = pl.BlockSpec((1, 1, s, Q_W), lambda b: (b, layer, 0, 0))
    n_in = 5
    return pl.pallas_call(
        functools.partial(_attn_ctx_body, lam_init=lam_init),
        grid=(nb,),
        in_specs=[
            pl.BlockSpec(lamp.shape, lambda b: (0, 0)),
            pl.BlockSpec((1, V_DIM), lambda b: (0, 0)),
            pl.BlockSpec((s, Q_W), lambda b: (b, 0)),
            pl.BlockSpec((s, Q_W), lambda b: (b, 1)),
            pl.BlockSpec((s, A_WIDTH), lambda b: (b, 2)),
        ] + ([] if caches is None else [pl.BlockSpec(memory_space=pl.ANY)] * 2),
        out_specs=[pl.BlockSpec((s, A_WIDTH), lambda b: (b, 0)), cache_blk, cache_blk],
        out_shape=[
            jax.ShapeDtypeStruct((nb * s, A_WIDTH), BF16),
            jax.ShapeDtypeStruct((nb, depth, s, Q_W), F32),
            jax.ShapeDtypeStruct((nb, depth, s, A_WIDTH), F32),
        ],
        input_output_aliases={} if caches is None else {n_in: 1, n_in + 1: 2},
        compiler_params=_cparams(("arbitrary",)),
        name="attn_ctx",
    )(lamp, lng, z, z, z, *(() if caches is None else caches))


def _rope(x, c, sa, sb):
    return x * c + pltpu.roll(x, HEAD_W - N_FREQ, 1) * sa + pltpu.roll(x, N_FREQ, 1) * sb


def _attn_lat_body(lamp_ref, lng_ref, q_ref, k_ref, v_ref, ck_ref, cv_ref,
                   cq_ref, saq_ref, sbq_ref, ckk_ref, sak_ref, sbk_ref,
                   o_ref, kall, vall, s_scr, e_scr, *, lam_init, kc, rc):
    ds = k_ref.shape[0]
    p = ck_ref.shape[2]
    tq = q_ref.shape[0]
    nch = (ds + p) // kc

    @pl.when(pl.program_id(2) == 0)
    def _():
        def ones_col(rows):
            return jnp.where(lax.broadcasted_iota(I32, (rows, LANES), 1) == 0, 1.0, 0.0).astype(BF16)

        def body(c, carry):
            r0 = pl.multiple_of(c * rc, rc)
            rs = pl.ds(r0, rc)
            kall[rs, :] = _rope(k_ref[rs, :], ckk_ref[rs, :], sak_ref[rs, :], sbk_ref[rs, :]).astype(BF16)
            vall[rs, 0:V_DIM] = v_ref[rs, :].astype(BF16)
            vall[rs, V_DIM:] = ones_col(rc)
            return carry

        lax.fori_loop(0, ds // rc, body, 0)
        kall[ds:ds + p, :] = ck_ref[0, 0].astype(BF16)
        vall[ds:ds + p, 0:V_DIM] = cv_ref[0, 0].astype(BF16)
        vall[ds:ds + p, V_DIM:] = ones_col(p)

    lam = _lam_value(lamp_ref, lam_init)
    q = _rope(q_ref[...], cq_ref[...], saq_ref[...], sbq_ref[...]) * (QK_SCALE * LOG2E)
    halves = _split_halves(q)
    chunks = [slice(c * kc, (c + 1) * kc) for c in range(nch)]

    def qk(n, ch, mrun):
        s = _dot_nt(halves[n], kall[ch, :])
        s_scr[n, :, ch] = s
        for t in range(kc // LANES):
            mrun = jnp.maximum(mrun, s[:, t * LANES:(t + 1) * LANES])
        return mrun

    def ex(n, ch, m):
        e_scr[n, :, ch] = jnp.exp2(s_scr[n, :, ch] - m).astype(BF16)

    neg = jnp.full((tq, LANES), -jnp.inf, F32)
    mrun = neg
    for ch in chunks:
        mrun = qk(0, ch, mrun)
    m0 = jnp.max(mrun, axis=-1, keepdims=True)
    mrun = neg
    for ch in chunks:
        mrun = qk(1, ch, mrun)
        ex(0, ch, m0)
    m1 = jnp.max(mrun, axis=-1, keepdims=True)
    oe0 = jnp.zeros((tq, V_DIM + LANES), F32)
    for ch in chunks:
        ex(1, ch, m1)
        oe0 = oe0 + _dot(e_scr[0, :, ch], vall[ch, :])
    oe1 = _dot(e_scr[1], vall[...])
    outs = [oe[:, 0:V_DIM] * (1.0 / oe[:, V_DIM:V_DIM + 1]) for oe in (oe0, oe1)]
    o = outs[0] - lam * outs[1]
    o_ref[...] = (_rms_lanes(o) * (lng_ref[...] * (1.0 - lam_init))).astype(o_ref.dtype)


def _attn_lat(z, cache_k, cache_v, rope_tabs, lamp, lng, *, layer, t_ctx, nb, ds, lam_init):
    p = cache_k.shape[2]
    tq = min(TQ_ATT, ds)
    sk = ds + p
    kc = 512 if sk % 512 == 0 else LANES
    assert t_ctx % ds == 0 and ds % tq == 0 and sk % kc == 0
    seq0, q0 = t_ctx // ds, t_ctx // tq
    nq = ds // tq
    qh, kh, vh = 0, Q_W // HEAD_W, 2 * Q_W // HEAD_W
    tab_q = pl.BlockSpec((tq, HEAD_W), lambda b, h, qi: (qi, 0))
    tab_k = pl.BlockSpec((ds, HEAD_W), lambda b, h, qi: (0, 0))
    cos, sa, sb = rope_tabs
    return pl.pallas_call(
        functools.partial(_attn_lat_body, lam_init=lam_init, kc=kc, rc=256),
        grid=(nb, A_HEADS, nq),
        in_specs=[
            pl.BlockSpec(lamp.shape, lambda b, h, qi: (0, 0)),
            pl.BlockSpec((1, V_DIM), lambda b, h, qi: (0, 0)),
            pl.BlockSpec((tq, HEAD_W), lambda b, h, qi: (q0 + b * nq + qi, qh + h)),
            pl.BlockSpec((ds, HEAD_W), lambda b, h, qi: (seq0 + b, kh + h)),
            pl.BlockSpec((ds, HEAD_W), lambda b, h, qi: (seq0 + b, vh + h)),
            pl.BlockSpec((1, 1, p, HEAD_W), lambda b, h, qi: (b, layer, 0, h)),
            pl.BlockSpec((1, 1, p, V_DIM), lambda b, h, qi: (b, layer, 0, h)),
            tab_q, tab_q, tab_q, tab_k, tab_k, tab_k,
        ],
        out_specs=pl.BlockSpec((tq, V_DIM), lambda b, h, qi: (b * nq + qi, h)),
        out_shape=jax.ShapeDtypeStruct((nb * ds, A_WIDTH), BF16),
        scratch_shapes=[
            pltpu.VMEM((sk, HEAD_W), BF16),
            pltpu.VMEM((sk, V_DIM + LANES), BF16),
            pltpu.VMEM((2, tq, sk), F32),
            pltpu.VMEM((2, tq, sk), BF16),
        ],
        compiler_params=_cparams(("parallel", "parallel", "arbitrary")),
        name="attn_lat",
    )(lamp, lng, z, z, z, cache_k, cache_v, cos, sa, sb, cos, sa, sb)


def _rope_tables(n_tok):
    n_rows = n_tok // GRID_W
    row = jnp.repeat(jnp.arange(n_rows), GRID_W).astype(F32)
    col = jnp.tile(jnp.arange(GRID_W), n_rows).astype(F32)
    inv = 1.0 / (ROPE_BASE ** (jnp.arange(N_FREQ, dtype=F32) / N_FREQ))
    ang = jnp.stack([row[:, None] * inv, col[:, None] * inv], axis=1)
    cos, sin = jnp.cos(ang), jnp.sin(ang)
    zero = jnp.zeros_like(sin)

    def lanes(first, second):
        per_axis = jnp.concatenate([first, second], axis=-1)
        return jnp.tile(per_axis.reshape(n_tok, 2 * 2 * N_FREQ), (1, 2))

    return lanes(cos, cos), lanes(-sin, zero), lanes(zero, sin)


def _pack_pair(hi, lo):
    hb = lax.bitcast_convert_type(hi.astype(BF16).astype(F32), U32)
    lb = lax.bitcast_convert_type(lo.astype(BF16).astype(F32), U32)
    return hb | (lb >> 16)


def _unpack_pair(u):
    hi = lax.bitcast_convert_type(u & jnp.uint32(0xFFFF0000), F32)
    lo = lax.bitcast_convert_type(u << 16, F32)
    return hi.astype(BF16), lo.astype(BF16)


def _hi_lo(x):
    hi = x.astype(BF16)
    return hi, (x - hi.astype(F32)).astype(BF16)


def _mixpost_body(oac_ref, oal_ref, zb_ref, zc_ref, zp_ref, zn_ref, xc_ref, xl_ref, mod_ref, g2_ref,
                  wout_ref, gng_ref, ws_ref, gb_ref, pw_ref, ps_ref, band_ref, bandp_ref, bandn_ref,
                  rw_ref, rb_ref, tri_ref, xo_ref, h_ref, idx_ref, gate_ref, rank_ref, cnt_ref, cat_scr, carry,
                  *, n_ctx_tiles, tps_ctx, tps_lat, n_exp):
    i = pl.program_id(0)
    is_ctx = i < n_ctx_tiles
    pos = jnp.where(is_ctx, i % tps_ctx, (i - n_ctx_tiles) % tps_lat)
    first = pos == 0
    last = pos == jnp.where(is_ctx, tps_ctx, tps_lat) - 1

    cat_scr[:, 0:A_WIDTH] = jnp.where(is_ctx, oac_ref[...], oal_ref[...])

    zb = jax.nn.gelu(zb_ref[...])
    u, v = zb[:, :B_WIDTH], zb[:, B_WIDTH:]
    for g in range(B_GROUPS):
        gs = slice(g * B_CH, (g + 1) * B_CH)
        vn = (_rms_lanes(v[:, gs]) * gng_ref[:, gs]).astype(BF16)
        for n in range(TILE // CHUNK):
            rs = slice(n * CHUNK, (n + 1) * CHUNK)
            mixed = _dot(ws_ref[g], vn[rs]) + gb_ref[:, gs]
            cat_scr[rs, A_WIDTH + g * B_CH:A_WIDTH + (g + 1) * B_CH] = (u[rs, gs] * mixed).astype(BF16)

    xc = zc_ref[...]
    xc_parts = _hi_lo(xc)
    xp_parts = _hi_lo(zp_ref[...] * jnp.where(first, 0.0, 1.0))
    xn_parts = _hi_lo(zn_ref[...] * jnp.where(last, 0.0, 1.0))
    r = lax.broadcasted_iota(I32, (TILE, 1), 0)
    for g, w in enumerate(POOL_WINDOWS):
        gs = slice(g * C_CH, (g + 1) * C_CH)
        half = w // 2
        acc = jnp.zeros((TILE, C_CH), F32)
        for part in xc_parts:
            acc = acc + _dot(band_ref[g], part[:, gs])
        for part in xp_parts:
            acc = acc + _dot(bandp_ref[g], part[:, gs])
        for part in xn_parts:
            acc = acc + _dot(bandn_ref[g], part[:, gs])
        left = jnp.where(first, jnp.minimum(half, r), half)
        right = jnp.where(last, jnp.minimum(half - 1, TILE - 1 - r), half - 1)
        cnt = (left + right + 1).astype(F32)
        pooled = acc / cnt - xc[:, gs]
        y = _dot(pooled.astype(BF16), pw_ref[g]) * ps_ref[:, gs]
        c0 = A_WIDTH + B_WIDTH + g * C_CH
        cat_scr[:, c0:c0 + C_CH] = y.astype(BF16)

    mix = _dot(cat_scr[...], wout_ref[...])
    x_new = jnp.where(is_ctx, xc_ref[...], xl_ref[...]) + mod_ref[0, 2:3, :] * mix
    xo_ref[...] = x_new
    h2 = _norm_mod(x_new, g2_ref[...], mod_ref[0, 3:4, :], mod_ref[0, 4:5, :])
    half_d = h2.shape[1] // 2
    h_ref[...] = _pack_pair(h2[:, :half_d], h2[:, half_d:])
    h_hi, h_lo = _hi_lo(h2)
    logits = _dot(h_hi, rw_ref[0]) + (_dot(h_lo, rw_ref[0]) + _dot(h_hi, rw_ref[1])) + rb_ref[...]
    _route_tile(logits, tri_ref, idx_ref, gate_ref, rank_ref, cnt_ref, carry, n_exp)


def _band_matrices():
    r = np.arange(TILE)[:, None]
    cur, prev, nxt = [], [], []
    for w in POOL_WINDOWS:
        half = w // 2
        c = np.arange(TILE)[None, :]
        cur.append((c >= r - half) & (c <= r + half - 1))
        ch = np.arange(HALO)[None, :]
        prev.append(ch - HALO >= r - half)
        nxt.append(TILE + ch <= r + half - 1)
    to = lambda m: jnp.asarray(np.stack(m).astype(np.float32), dtype=BF16)
    return to(cur), to(prev), to(nxt)


def _mixpost(oa_ctx, oa_lat, z, x_ctx, x_lat, mod, g2, w_out, gng, ws, gb, pw, ps, rw, rb, *, s_ctx, ds,
             n_exp):
    t_ctx, d = x_ctx.shape
    t = t_ctx + x_lat.shape[0]
    nt = t // TILE
    n_ctx_tiles = t_ctx // TILE
    tps_ctx, tps_lat = s_ctx // TILE, ds // TILE
    assert s_ctx % TILE == 0 and ds % TILE == 0
    grp = functools.partial(_group_of_block, rows=TILE, t_ctx=t_ctx, ds=ds)
    band, bandp, bandn = _band_matrices()
    zb_blk = (2 * Q_W + A_WIDTH) // (2 * B_WIDTH)
    zc_blk = (2 * Q_W + A_WIDTH + 2 * B_WIDTH) // C_WIDTH
    hpt = TILE // HALO
    const2 = lambda i: (0, 0)
    const3 = lambda i: (0, 0, 0)
    lane_blk = pl.BlockSpec((TILE, LANES), lambda i: (i, 0))
    tri = jnp.asarray(np.tril(np.ones((TILE, TILE), np.float32), -1), dtype=BF16)
    return pl.pallas_call(
        functools.partial(_mixpost_body, n_ctx_tiles=n_ctx_tiles, tps_ctx=tps_ctx, tps_lat=tps_lat,
                          n_exp=n_exp),
        grid=(nt,),
        in_specs=_pair_specs(TILE, A_WIDTH, n_ctx_tiles) + [
            pl.BlockSpec((TILE, 2 * B_WIDTH), lambda i: (i, zb_blk)),
            pl.BlockSpec((TILE, C_WIDTH), lambda i: (i, zc_blk)),
            pl.BlockSpec((HALO, C_WIDTH), lambda i: (jnp.maximum(i * hpt - 1, 0), zc_blk)),
            pl.BlockSpec((HALO, C_WIDTH), lambda i: (jnp.minimum((i + 1) * hpt, nt * hpt - 1), zc_blk)),
        ] + _pair_specs(TILE, d, n_ctx_tiles) + [
            pl.BlockSpec((1, 6, d), lambda i: (grp(i), 0, 0)),
            pl.BlockSpec((1, d), const2),
            pl.BlockSpec(w_out.shape, const2),
            pl.BlockSpec((1, B_WIDTH), const2),
            pl.BlockSpec(ws.shape, const3),
            pl.BlockSpec(gb.shape, const2),
            pl.BlockSpec(pw.shape, const3),
            pl.BlockSpec((1, C_WIDTH), const2),
            pl.BlockSpec(band.shape, const3),
            pl.BlockSpec(bandp.shape, const3),
            pl.BlockSpec(bandn.shape, const3),
            pl.BlockSpec(rw.shape, const3),
            pl.BlockSpec(rb.shape, const2),
            pl.BlockSpec((TILE, TILE), const2),
        ],
        out_specs=[
            pl.BlockSpec((TILE, d), lambda i: (i, 0)),
            pl.BlockSpec((TILE, d // 2), lambda i: (i, 0)),
            lane_blk, lane_blk, lane_blk,
            pl.BlockSpec((1, LANES), const2),
        ],
        out_shape=[
            jax.ShapeDtypeStruct((t, d), F32),
            jax.ShapeDtypeStruct((t, d // 2), U32),
            jax.ShapeDtypeStruct((t, LANES), I32),
            jax.ShapeDtypeStruct((t, LANES), F32),
            jax.ShapeDtypeStruct((t, LANES), I32),
            jax.ShapeDtypeStruct((1, LANES), F32),
        ],
        scratch_shapes=[pltpu.VMEM((TILE, d), BF16), pltpu.VMEM((1, LANES), F32)],
        compiler_params=_cparams(("arbitrary",)),
        name="mixpost",
    )(oa_ctx, oa_lat, z, z, z, z, x_ctx, x_lat, mod, g2, w_out, gng, ws, gb, pw, ps, band, bandp, bandn, rw, rb,
      tri)


def _route_tile(logits, tri_ref, idx_ref, gate_ref, rank_ref, cnt_ref, carry, n_exp):
    @pl.when(pl.program_id(0) == 0)
    def _():
        carry[...] = jnp.zeros_like(carry)

    lane = lax.broadcasted_iota(I32, logits.shape, 1).astype(F32)
    l = jnp.where(lane < n_exp, logits, -jnp.inf)
    vals, idxs, sels = [], [], []
    for _ in range(TOP_K):
        m = jnp.max(l, axis=-1, keepdims=True)
        idx = jnp.min(jnp.where(l == m, lane, float(LANES)), axis=-1, keepdims=True)
        sel = lane == idx
        vals.append(m)
        idxs.append(idx)
        sels.append(sel)
        l = jnp.where(sel, -jnp.inf, l)
    chosen = functools.reduce(jnp.logical_or, sels)
    chosen_f = jnp.where(chosen, 1.0, 0.0)
    prefix = carry[...] + _dot(tri_ref[...], chosen_f.astype(BF16))
    es = [jnp.exp(v - vals[0]) for v in vals]
    tot = functools.reduce(jnp.add, es)
    idx_o = jnp.zeros(logits.shape, F32)
    gate_o = jnp.zeros(logits.shape, F32)
    rank_o = jnp.zeros(logits.shape, F32)
    for k in range(TOP_K):
        rank_k = jnp.sum(jnp.where(sels[k], prefix, 0.0), axis=-1, keepdims=True)
        idx_o = jnp.where(lane == k, idxs[k], idx_o)
        gate_o = jnp.where(lane == k, es[k] / tot, gate_o)
        rank_o = jnp.where(lane == k, rank_k, rank_o)
    idx_ref[...] = idx_o.astype(I32)
    gate_ref[...] = gate_o
    rank_ref[...] = rank_o.astype(I32)
    carry[...] = carry[...] + jnp.sum(chosen_f, axis=0, keepdims=True)
    cnt_ref[...] = carry[...]


DMA_UNROLL = 8


def _dispatch_body(fill_ref, dest_ref, h_ref, xb_ref, zbuf, sem, zsem, *, n_exp):
    rows = h_ref.shape[0]

    @pl.when(pl.program_id(0) == 0)
    def _():
        zbuf[...] = jnp.zeros_like(zbuf)

        def fill_copy(e):
            start = pl.multiple_of(fill_ref[0, e], SUB_MOE)
            return pltpu.make_async_copy(zbuf, xb_ref.at[pl.ds(start, SUB_MOE)], zsem)

        for e in range(n_exp):
            pl.when(fill_ref[1, e] > 0)(lambda e=e: fill_copy(e).start())
        for e in range(n_exp):
            pl.when(fill_ref[1, e] > 0)(lambda e=e: fill_copy(e).wait())

    def row_copy(r, d):
        return pltpu.make_async_copy(h_ref.at[pl.ds(r, 1)], xb_ref.at[pl.ds(d, 1)], sem)

    def for_rows(fn):
        def body(g, carry):
            r0 = pl.multiple_of(g * DMA_UNROLL, DMA_UNROLL)
            for u in range(DMA_UNROLL):
                for k in range(TOP_K):
                    fn(row_copy(r0 + u, dest_ref[0, 0, (r0 + u) * TOP_K + k]))
            return carry

        lax.fori_loop(0, rows // DMA_UNROLL, body, 0)

    for_rows(lambda cp: cp.start())
    for_rows(lambda cp: cp.wait())


def _dispatch(fill, dest_tiles, h_packed, n_slots):
    t, w = h_packed.shape
    return pl.pallas_call(
        functools.partial(_dispatch_body, n_exp=fill.shape[1]),
        grid=(t // TILE,),
        in_specs=[
            pl.BlockSpec(memory_space=pltpu.SMEM),
            pl.BlockSpec((1, 1, TILE * TOP_K), lambda i: (i, 0, 0), memory_space=pltpu.SMEM),
            pl.BlockSpec((TILE, w), lambda i: (i, 0)),
        ],
        out_specs=pl.BlockSpec(memory_space=pl.ANY),
        out_shape=jax.ShapeDtypeStruct((n_slots, w), U32),
        scratch_shapes=[pltpu.VMEM((SUB_MOE, w), U32), pltpu.SemaphoreType.DMA(()), pltpu.SemaphoreType.DMA(())],
        compiler_params=_cparams(("arbitrary",)),
        name="dispatch",
    )(fill, dest_tiles, h_packed)


def _moe_body(be_ref, bv_ref, nu_ref, xb_ref, wg_ref, wu_ref, wd_ref, bg_ref, bu_ref, bd_ref,
              y_ref, x_scr):
    del be_ref, nu_ref
    i, j = pl.program_id(0), pl.program_id(1)
    valid = bv_ref[i]
    tm = xb_ref.shape[0]
    half_d = xb_ref.shape[1]
    n_sub = (valid + SUB_MOE - 1) // SUB_MOE

    @pl.when(valid > 0)
    def _():
        @pl.when(j == 0)
        def _():
            def unpack(s, carry):
                rs = pl.ds(pl.multiple_of(s * SUB_MOE, SUB_MOE), SUB_MOE)
                hi, lo = _unpack_pair(xb_ref[rs, :])
                x_scr[rs, 0:half_d] = hi
                x_scr[rs, half_d:] = lo
                return carry

            def init(s, carry):
                rs = pl.ds(pl.multiple_of(s * SUB_MOE, SUB_MOE), SUB_MOE)
                y_ref[rs, :] = jnp.broadcast_to(bd_ref[0, 0], (SUB_MOE, y_ref.shape[1]))
                return carry

            lax.fori_loop(0, n_sub, unpack, 0)
            lax.fori_loop(0, tm // SUB_MOE, init, 0)

        def ffn(rows):
            x = x_scr[0:rows, :]
            g = _dot(x, wg_ref[0, 0].astype(BF16)) + bg_ref[0, 0]
            u = _dot(x, wu_ref[0, 0].astype(BF16)) + bu_ref[0, 0]
            g = jnp.minimum(g, SWIGLU_LIMIT)
            u = jnp.clip(u, -SWIGLU_LIMIT, SWIGLU_LIMIT)
            a = g * jax.nn.sigmoid(SWIGLU_ALPHA * g) * (u + 1.0)
            y_ref[0:rows, :] += _dot(a.astype(BF16), wd_ref[0, 0].astype(BF16))

        for k in range(1, tm // SUB_MOE + 1):
            pl.when(n_sub == k)(functools.partial(ffn, k * SUB_MOE))


def _moe(blk_e, blk_valid, n_used, xb, w_gu, b_gu, w_dn, b_dn, *, layer):
    n_slots, half_d = xb.shape
    d = 2 * half_d
    n_exp, _, two_f = w_gu.shape[1:]
    f = two_f // 2
    tm, tf = TM_MOE, TF_MOE
    nf = f // tf
    nb = n_slots // tm

    def live(i, nu):
        return i < nu[0]

    def row_blk(i, j, be, bv, nu):
        return (jnp.minimum(i, nu[0] - 1), 0)

    def jj(i, j, nu):
        return jnp.where(live(i, nu), j, nf - 1)

    grid_spec = pltpu.PrefetchScalarGridSpec(
        num_scalar_prefetch=3,
        grid=(nb, nf),
        in_specs=[
            pl.BlockSpec((tm, half_d), row_blk),
            pl.BlockSpec((1, 1, d, tf), lambda i, j, be, bv, nu: (layer, be[i], 0, jj(i, j, nu))),
            pl.BlockSpec((1, 1, d, tf), lambda i, j, be, bv, nu: (layer, be[i], 0, nf + jj(i, j, nu))),
            pl.BlockSpec((1, 1, tf, d), lambda i, j, be, bv, nu: (layer, be[i], jj(i, j, nu), 0)),
            pl.BlockSpec((1, 1, 1, tf), lambda i, j, be, bv, nu: (layer, be[i], 0, jj(i, j, nu))),
            pl.BlockSpec((1, 1, 1, tf), lambda i, j, be, bv, nu: (layer, be[i], 0, nf + jj(i, j, nu))),
            pl.BlockSpec((1, 1, 1, d), lambda i, j, be, bv, nu: (layer, be[i], 0, 0)),
        ],
        out_specs=pl.BlockSpec((tm, d), row_blk),
        scratch_shapes=[pltpu.VMEM((tm, d), BF16)],
    )
    nl = w_gu.shape[0]
    return pl.pallas_call(
        _moe_body,
        grid_spec=grid_spec,
        out_shape=jax.ShapeDtypeStruct((n_slots, d), F32),
        compiler_params=_cparams(("arbitrary", "arbitrary")),
        name="moe",
    )(blk_e, blk_valid, n_used, xb, w_gu, w_gu, w_dn,
      b_gu.reshape(nl, n_exp, 1, two_f), b_gu.reshape(nl, n_exp, 1, two_f), b_dn.reshape(nl, n_exp, 1, d))


def _combine_body(dest_ref, destn_ref, gate_ref, x_ref, mod_ref, fg_ref, yb_ref, oc_ref, ol_ref, buf, sems,
                  *, final, n_ctx_tiles):
    rows = x_ref.shape[0]
    i = pl.program_id(0)
    cur = i % 2

    def row_copy(b, r, k, d):
        return pltpu.make_async_copy(yb_ref.at[pl.ds(d, 1)], buf.at[b, k, pl.ds(r, 1)], sems.at[b])

    def for_rows(b, d_ref, fn):
        def body(g, carry):
            r0 = pl.multiple_of(g * DMA_UNROLL, DMA_UNROLL)
            for u in range(DMA_UNROLL):
                for k in range(TOP_K):
                    fn(row_copy(b, r0 + u, k, d_ref[0, 0, (r0 + u) * TOP_K + k]))
            return carry

        lax.fori_loop(0, rows // DMA_UNROLL, body, 0)

    @pl.when(i == 0)
    def _():
        for_rows(0, dest_ref, lambda cp: cp.start())

    @pl.when(i + 1 < pl.num_programs(0))
    def _():
        for_rows(1 - cur, destn_ref, lambda cp: cp.start())

    for_rows(cur, dest_ref, lambda cp: cp.wait())

    ff = gate_ref[:, 0:1] * buf[cur, 0]
    for k in range(1, TOP_K):
        ff = ff + gate_ref[:, k:k + 1] * buf[cur, k]
    x_new = x_ref[...] + mod_ref[0, 5:6, :] * ff
    if final:
        x_new = _rms_lanes(x_new) * fg_ref[...]
    is_ctx = pl.program_id(0) < n_ctx_tiles

    @pl.when(is_ctx)
    def _():
        oc_ref[...] = x_new

    @pl.when(jnp.logical_not(is_ctx))
    def _():
        ol_ref[...] = x_new


def _combine(dest_tiles, gates, x, mod, final_g, yb, *, t_ctx, ds, final):
    t, d = x.shape
    n_ctx_tiles = t_ctx // TILE
    grp = functools.partial(_group_of_block, rows=TILE, t_ctx=t_ctx, ds=ds)
    nt = t // TILE
    return pl.pallas_call(
        functools.partial(_combine_body, final=final, n_ctx_tiles=n_ctx_tiles),
        grid=(nt,),
        in_specs=[
            pl.BlockSpec((1, 1, TILE * TOP_K), lambda i: (i, 0, 0), memory_space=pltpu.SMEM),
            pl.BlockSpec((1, 1, TILE * TOP_K), lambda i: (jnp.minimum(i + 1, nt - 1), 0, 0),
                         memory_space=pltpu.SMEM),
            pl.BlockSpec((TILE, LANES), lambda i: (i, 0)),
            pl.BlockSpec((TILE, d), lambda i: (i, 0)),
            pl.BlockSpec((1, 6, d), lambda i: (grp(i), 0, 0)),
            pl.BlockSpec((1, d), lambda i: (0, 0)),
            pl.BlockSpec(memory_space=pl.ANY),
        ],
        out_specs=_pair_specs(TILE, d, n_ctx_tiles),
        out_shape=[jax.ShapeDtypeStruct((t_ctx, d), F32), jax.ShapeDtypeStruct((t - t_ctx, d), F32)],
        scratch_shapes=[pltpu.VMEM((2, TOP_K, TILE, d), F32), pltpu.SemaphoreType.DMA((2,))],
        compiler_params=_cparams(("arbitrary",)),
        name="combine",
    )(dest_tiles, dest_tiles, gates, x, mod, final_g, yb)


def _slot_layout(idx, rank, counts, n_blocks):
    n_exp = counts.shape[0]
    n_blk = (counts + TM_MOE - 1) // TM_MOE
    padded = n_blk * TM_MOE
    pend = jnp.cumsum(padded)
    pstart = pend - padded
    per = jnp.maximum(-(-counts // jnp.maximum(n_blk, 1)), 1)
    per = (per + SUB_MOE - 1) // SUB_MOE * SUB_MOE
    per_tok = per[idx]
    blk_in = jnp.floor((rank.astype(F32) + 0.5) / per_tok.astype(F32)).astype(I32)
    dest = pstart[idx] + blk_in * TM_MOE + (rank - blk_in * per_tok)
    bstart = jnp.arange(n_blocks, dtype=I32) * TM_MOE
    n_used = (pend[-1] // TM_MOE).astype(I32).reshape(1)
    owner_of = jnp.minimum(bstart, jnp.maximum(pend[-1] - TM_MOE, 0))
    blk_e = jnp.minimum(jnp.searchsorted(pend, owner_of, side='right'), n_exp - 1).astype(I32)
    local = (bstart - pstart[blk_e]) // TM_MOE
    blk_valid = jnp.clip(counts[blk_e] - local * per[blk_e], 0, per[blk_e])
    blk_valid = jnp.where(bstart < pend[-1], blk_valid, 0).astype(I32)
    last_blk = jnp.maximum(counts - 1, 0) // per
    last_cnt = counts - last_blk * per
    last_group = pstart + last_blk * TM_MOE + jnp.maximum((last_cnt + SUB_MOE - 1) // SUB_MOE - 1, 0) * SUB_MOE
    fill = jnp.stack([last_group, (counts > 0).astype(I32)]).astype(I32)
    return dest.astype(I32), fill, blk_e, blk_valid, n_used


def kernel(x_prompt, x_sample, cache_k, cache_v, c, c_ctx, norm1_g, norm2_g, w_ada, b_ada, w_in, w_out,
           lam_q1, lam_k1, lam_q2, lam_k2, diff_ln_g, gmlp_norm_g, gmlp_ws, gmlp_bs, pool_w, pool_scale,
           router_w, router_b, exp_w_gu, exp_b_gu, exp_w_down, exp_b_down, final_g):
    nb_ctx, s_ctx, d = x_prompt.shape
    nb_lat, ds, _ = x_sample.shape
    depth = w_in.shape[0]
    n_exp = router_w.shape[-1]
    p_len = cache_k.shape[2]
    t_ctx, t_lat = nb_ctx * s_ctx, nb_lat * ds
    t = t_ctx + t_lat
    assert t_ctx % TILE == 0 and t_lat % TILE == 0 and (t * TOP_K) % TM_MOE == 0

    x_ctx, x_lat = x_prompt.reshape(t_ctx, d), x_sample.reshape(t_lat, d)
    cond_t = jnp.concatenate([c_ctx[None, :], c], axis=0).T
    mod = _ada(cond_t, w_ada, b_ada)
    mod = mod.reshape(depth, 1 + nb_lat, 6, d)

    rope_tabs = _rope_tables(ds)
    ck = cache_k.reshape(nb_lat, depth, p_len, A_HEADS * HEAD_W)
    cv = cache_v.reshape(nb_lat, depth, p_len, A_WIDTH)
    w_in_b = w_in.astype(BF16)
    w_out_b = w_out.astype(BF16)
    ws_b = gmlp_ws.astype(BF16)
    pw_b = pool_w.astype(BF16)
    rw_pad = jnp.pad(router_w, ((0, 0), (0, 0), (0, LANES - n_exp)))
    rw_hi = rw_pad.astype(BF16)
    rw_parts = jnp.stack([rw_hi, (rw_pad - rw_hi.astype(F32)).astype(BF16)], axis=1)
    rb_pad = jnp.pad(router_b, ((0, 0), (0, LANES - n_exp)))
    n_blocks = t * TOP_K // TM_MOE + n_exp

    caches = None
    for l in range(depth):
        lam_init = 0.8 - 0.6 * math.exp(-0.3 * l)
        lamp = jnp.stack([lam_q1[l], lam_k1[l], lam_q2[l], lam_k2[l]])
        lng = diff_ln_g[l][None, :]
        z = _inproj(x_ctx, x_lat, mod[l], norm1_g[l][None, :], w_in_b[l], ds=ds)
        oa_ctx, *caches = _attn_ctx(z, lamp, lng, caches, layer=l, depth=depth, nb=nb_ctx, s=s_ctx,
                                    lam_init=lam_init)
        oa_lat = _attn_lat(z, ck, cv, rope_tabs, lamp, lng, layer=l, t_ctx=t_ctx, nb=nb_lat, ds=ds,
                           lam_init=lam_init)
        gb = jnp.repeat(gmlp_bs[l].T, B_CH, axis=1)
        x_mid, h_packed, idx, gates, rank, counts = _mixpost(
            oa_ctx, oa_lat, z, x_ctx, x_lat, mod[l], norm2_g[l][None, :], w_out_b[l], gmlp_norm_g[l][None, :],
            ws_b[l], gb, pw_b[l], pool_scale[l][None, :], rw_parts[l], rb_pad[l][None, :], s_ctx=s_ctx, ds=ds,
            n_exp=n_exp)
        dest, fill, blk_e, blk_valid, n_used = _slot_layout(
            idx[:, :TOP_K], rank[:, :TOP_K], counts[0, :n_exp].astype(I32), n_blocks)
        dest_tiles = dest.reshape(t // TILE, 1, TILE * TOP_K)
        xb = _dispatch(fill, dest_tiles, h_packed, n_blocks * TM_MOE)
        yb = _moe(blk_e, blk_valid, n_used, xb, exp_w_gu, exp_b_gu, exp_w_down, exp_b_down, layer=l)
        x_ctx, x_lat = _combine(dest_tiles, gates, x_mid, mod[l], final_g[None, :], yb, t_ctx=t_ctx, ds=ds,
                                final=(l == depth - 1))

    return (x_ctx.reshape(nb_ctx, s_ctx, d), x_lat.reshape(nb_lat, ds, d),
            caches[0].reshape(nb_ctx, depth, s_ctx, A_HEADS, HEAD_W),
            caches[1].reshape(nb_ctx, depth, s_ctx, A_HEADS, V_DIM))
```

```python
import functools
import math

import numpy as np
import jax
import jax.numpy as jnp
from jax import lax
from jax.experimental import pallas as pl
from jax.experimental.pallas import tpu as pltpu

F32 = jnp.float32
BF16 = jnp.bfloat16
I32 = jnp.int32
U32 = jnp.uint32

GRID_W = 64
A_HEADS = 8
QK_DIM = 64
V_DIM = 2 * QK_DIM
HEAD_W = 2 * QK_DIM
Q_W = A_HEADS * HEAD_W
A_WIDTH = A_HEADS * V_DIM
B_GROUPS = 4
CHUNK = 128
B_CH = 128
B_WIDTH = B_GROUPS * B_CH
POOL_WINDOWS = (2, 4, 8, 16)
N_POOL = len(POOL_WINDOWS)
C_CH = 128
C_WIDTH = N_POOL * C_CH
ROPE_BASE = 10000.0
N_FREQ = QK_DIM // 4
TOP_K = 4
SWIGLU_ALPHA = 1.702
SWIGLU_LIMIT = 7.0
EPS = 1e-6
QK_SCALE = QK_DIM ** -0.5

LANES = 128
TILE = 256
HALO = 16
TM_PROJ_CHOICES = (1024, 512, 256)
LOG2E = 1.4426950408889634
TM_MOE = 1024
SUB_MOE = 64
TF_MOE = 256
TQ_ATT = 512
VMEM_LIMIT = 56 * 1024 * 1024


def _cparams(sem, vmem=VMEM_LIMIT):
    return pltpu.CompilerParams(dimension_semantics=sem, vmem_limit_bytes=vmem)


def _norm_mod(x, g, shift, scale):
    ms = jnp.mean(x * x, axis=-1, keepdims=True)
    return x * lax.rsqrt(ms + EPS) * g * (1.0 + scale) + shift


def _rms_lanes(x):
    return x * lax.rsqrt(jnp.mean(x * x, axis=-1, keepdims=True) + EPS)


def _dot(a, b):
    return jnp.dot(a, b, preferred_element_type=F32)


def _dot_nt(a, b):
    return lax.dot_general(a, b, (((1,), (1,)), ((), ())), preferred_element_type=F32)


def _ada_body(ct_ref, w_ref, b_ref, o_ref, sb, *, kc):
    d, ng = ct_ref.shape
    tn = w_ref.shape[-1]

    @pl.when(jnp.logical_and(pl.program_id(0) == 0, pl.program_id(1) == 0))
    def _():
        def fill(c, carry):
            k0 = pl.multiple_of(c * kc, kc)
            s = jax.nn.silu(ct_ref[pl.ds(k0, kc), :])
            for gi in range(ng):
                sb[gi, pl.ds(k0, kc), :] = jnp.broadcast_to(s[:, gi:gi + 1], (kc, LANES))
            return carry

        lax.fori_loop(0, d // kc, fill, 0)

    def body(c, accs):
        k0 = pl.multiple_of(c * kc, kc)
        w = w_ref[0, pl.ds(k0, kc), :]
        out = []
        for gi, acc in enumerate(accs):
            s = sb[gi, pl.ds(k0, kc), :]
            parts = [jnp.sum((w[:, t * LANES:(t + 1) * LANES] * s).reshape(kc // 8, 8, LANES), axis=0)
                     for t in range(tn // LANES)]
            out.append(acc + jnp.concatenate(parts, axis=1))
        return tuple(out)

    accs = lax.fori_loop(0, d // kc, body, tuple(jnp.zeros((8, tn), F32) for _ in range(ng)))
    for gi in range(ng):
        o_ref[0, gi:gi + 1, :] = jnp.sum(accs[gi], axis=0, keepdims=True) + b_ref[0]


def _ada(cond_t, w_ada, b_ada):
    d, ng = cond_t.shape
    nl, _, n = w_ada.shape
    tn = 1024
    return pl.pallas_call(
        functools.partial(_ada_body, kc=32),
        grid=(nl, n // tn),
        in_specs=[
            pl.BlockSpec((d, ng), lambda l, j: (0, 0)),
            pl.BlockSpec((1, d, tn), lambda l, j: (l, 0, j)),
            pl.BlockSpec((1, 1, tn), lambda l, j: (l, 0, j)),
        ],
        out_specs=pl.BlockSpec((1, ng, tn), lambda l, j: (l, 0, j)),
        out_shape=jax.ShapeDtypeStruct((nl, ng, n), F32),
        scratch_shapes=[pltpu.VMEM((ng, d, LANES), F32)],
        compiler_params=_cparams(("arbitrary", "arbitrary")),
        name="ada",
    )(cond_t, w_ada, b_ada.reshape(nl, 1, n))


def _inproj_body(xc_ref, xl_ref, mod_ref, g_ref, w_ref, z_ref, h_scr, *, rc, n_ctx_blocks):
    @pl.when(pl.program_id(1) == 0)
    def _():
        shift = mod_ref[0, 0:1, :]
        scale = mod_ref[0, 1:2, :]
        g = g_ref[...]

        def fill(x_ref):
            def body(c, carry):
                r0 = pl.multiple_of(c * rc, rc)
                h = _norm_mod(x_ref[pl.ds(r0, rc), :], g, shift, scale)
                h_scr[pl.ds(r0, rc), :] = h.astype(BF16)
                return carry

            lax.fori_loop(0, x_ref.shape[0] // rc, body, 0)

        is_ctx = pl.program_id(0) < n_ctx_blocks
        pl.when(is_ctx)(lambda: fill(xc_ref))
        pl.when(jnp.logical_not(is_ctx))(lambda: fill(xl_ref))

    z_ref[...] = _dot(h_scr[...], w_ref[...])


def _group_of_block(i, rows, t_ctx, ds):
    r = i * rows
    return jnp.where(r < t_ctx, 0, 1 + (r - t_ctx) // ds)


def _pair_specs(rows, width, n_ctx_blocks):
    return [
        pl.BlockSpec((rows, width), lambda i, *_: (jnp.minimum(i, n_ctx_blocks - 1), 0)),
        pl.BlockSpec((rows, width), lambda i, *_: (jnp.maximum(i - n_ctx_blocks, 0), 0)),
    ]


def _inproj(x_ctx, x_lat, mod, g1, w_in, *, ds):
    t_ctx, d = x_ctx.shape
    t = t_ctx + x_lat.shape[0]
    n = w_in.shape[1]
    tm = next(m for m in TM_PROJ_CHOICES if t_ctx % m == 0 and ds % m == 0)
    tn = 768
    assert n % tn == 0
    grp = functools.partial(_group_of_block, rows=tm, t_ctx=t_ctx, ds=ds)
    return pl.pallas_call(
        functools.partial(_inproj_body, rc=64, n_ctx_blocks=t_ctx // tm),
        grid=(t // tm, n // tn),
        in_specs=_pair_specs(tm, d, t_ctx // tm) + [
            pl.BlockSpec((1, 6, d), lambda i, j: (grp(i), 0, 0)),
            pl.BlockSpec((1, d), lambda i, j: (0, 0)),
            pl.BlockSpec((d, tn), lambda i, j: (0, j)),
        ],
        out_specs=pl.BlockSpec((tm, tn), lambda i, j: (i, j)),
        out_shape=jax.ShapeDtypeStruct((t, n), F32),
        scratch_shapes=[pltpu.VMEM((tm, d), BF16)],
        compiler_params=_cparams(("arbitrary", "arbitrary")),
        name="inproj",
    )(x_ctx, x_lat, mod, g1, w_in)


def _lam_value(lamp_ref, lam_init):
    lp = lamp_ref[...]
    a = jnp.sum(lp[0:1] * lp[1:2], keepdims=True)
    b = jnp.sum(lp[2:3] * lp[3:4], keepdims=True)
    return jnp.exp(a) - jnp.exp(b) + lam_init


def _split_halves(q):
    first = lax.broadcasted_iota(I32, (1, HEAD_W), 1) < QK_DIM
    return jnp.where(first, q, 0.0).astype(BF16), jnp.where(first, 0.0, q).astype(BF16)


def _attn_ctx_body(lamp_ref, lng_ref, q_ref, k_ref, v_ref, *rest, lam_init):
    o_ref, ko_ref, vo_ref = rest[-3:]
    lam = _lam_value(lamp_ref, lam_init)
    g = lng_ref[...] * (1.0 - lam_init)
    ko_ref[0, 0] = k_ref[...]
    vo_ref[0, 0] = v_ref[...]
    for h in range(A_HEADS):
        hs = slice(h * HEAD_W, (h + 1) * HEAD_W)
        k = k_ref[:, hs].astype(BF16)
        v = v_ref[:, hs].astype(BF16)
        ps = []
        for qh in _split_halves(q_ref[:, hs] * QK_SCALE):
            s = _dot_nt(qh, k)
            e = jnp.exp(s - jnp.max(s, axis=-1, keepdims=True))
            ps.append(e * (1.0 / jnp.sum(e, axis=-1, keepdims=True)))
        o = _dot((ps[0] - lam * ps[1]).astype(BF16), v)
        o_ref[:, hs] = (_rms_lanes(o) * g).astype(o_ref.dtype)


def _attn_ctx(z, lamp, lng, caches, *, layer, depth, nb, s, lam_init):
    cache_blk = pl.BlockSpec((1, 1, s, Q_W), lambda b: (b, layer, 0, 0))
    n_in = 5
    return pl.pallas_call(
        functools.partial(_attn_ctx_body, lam_init=lam_init),
        grid=(nb,),
        in_specs=[
            pl.BlockSpec(lamp.shape, lambda b: (0, 0)),
            pl.BlockSpec((1, V_DIM), lambda b: (0, 0)),
            pl.BlockSpec((s, Q_W), lambda b: (b, 0)),
            pl.BlockSpec((s, Q_W), lambda b: (b, 1)),
            pl.BlockSpec((s, A_WIDTH), lambda b: (b, 2)),
        ] + ([] if caches is None else [pl.BlockSpec(memory_space=pl.ANY)] * 2),
        out_specs=[pl.BlockSpec((s, A_WIDTH), lambda b: (b, 0)), cache_blk, cache_blk],
        out_shape=[
            jax.ShapeDtypeStruct((nb * s, A_WIDTH), BF16),
            jax.ShapeDtypeStruct((nb, depth, s, Q_W), F32),
            jax.ShapeDtypeStruct((nb, depth, s, A_WIDTH), F32),
        ],
        input_output_aliases={} if caches is None else {n_in: 1, n_in + 1: 2},
        compiler_params=_cparams(("arbitrary",)),
        name="attn_ctx",
    )(lamp, lng, z, z, z, *(() if caches is None else caches))


def _rope(x, c, sa, sb):
    return x * c + pltpu.roll(x, HEAD_W - N_FREQ, 1) * sa + pltpu.roll(x, N_FREQ, 1) * sb


def _attn_lat_body(lamp_ref, lng_ref, q_ref, k_ref, v_ref, ck_ref, cv_ref,
                   cq_ref, saq_ref, sbq_ref, ckk_ref, sak_ref, sbk_ref,
                   o_ref, kall, vall, s_scr, e_scr, *, lam_init, kc, rc):
    ds = k_ref.shape[0]
    p = ck_ref.shape[2]
    tq = q_ref.shape[0]
    nch = (ds + p) // kc

    @pl.when(pl.program_id(2) == 0)
    def _():
        def ones_col(rows):
            return jnp.where(lax.broadcasted_iota(I32, (rows, LANES), 1) == 0, 1.0, 0.0).astype(BF16)

        def body(c, carry):
            r0 = pl.multiple_of(c * rc, rc)
            rs = pl.ds(r0, rc)
            kall[rs, :] = _rope(k_ref[rs, :], ckk_ref[rs, :], sak_ref[rs, :], sbk_ref[rs, :]).astype(BF16)
            vall[rs, 0:V_DIM] = v_ref[rs, :].astype(BF16)
            vall[rs, V_DIM:] = ones_col(rc)
            return carry

        lax.fori_loop(0, ds // rc, body, 0)
        kall[ds:ds + p, :] = ck_ref[0, 0].astype(BF16)
        vall[ds:ds + p, 0:V_DIM] = cv_ref[0, 0].astype(BF16)
        vall[ds:ds + p, V_DIM:] = ones_col(p)

    lam = _lam_value(lamp_ref, lam_init)
    q = _rope(q_ref[...], cq_ref[...], saq_ref[...], sbq_ref[...]) * (QK_SCALE * LOG2E)
    halves = _split_halves(q)
    chunks = [slice(c * kc, (c + 1) * kc) for c in range(nch)]

    def qk(n, ch, mrun):
        s = _dot_nt(halves[n], kall[ch, :])
        s_scr[n, :, ch] = s
        for t in range(kc // LANES):
            mrun = jnp.maximum(mrun, s[:, t * LANES:(t + 1) * LANES])
        return mrun

    def ex(n, ch, m):
        e_scr[n, :, ch] = jnp.exp2(s_scr[n, :, ch] - m).astype(BF16)

    neg = jnp.full((tq, LANES), -jnp.inf, F32)
    mrun = neg
    for ch in chunks:
        mrun = qk(0, ch, mrun)
    m0 = jnp.max(mrun, axis=-1, keepdims=True)
    mrun = neg
    for ch in chunks:
        mrun = qk(1, ch, mrun)
        ex(0, ch, m0)
    m1 = jnp.max(mrun, axis=-1, keepdims=True)
    oe0 = jnp.zeros((tq, V_DIM + LANES), F32)
    for ch in chunks:
        ex(1, ch, m1)
        oe0 = oe0 + _dot(e_scr[0, :, ch], vall[ch, :])
    oe1 = _dot(e_scr[1], vall[...])
    outs = [oe[:, 0:V_DIM] * (1.0 / oe[:, V_DIM:V_DIM + 1]) for oe in (oe0, oe1)]
    o = outs[0] - lam * outs[1]
    o_ref[...] = (_rms_lanes(o) * (lng_ref[...] * (1.0 - lam_init))).astype(o_ref.dtype)


def _attn_lat(z, cache_k, cache_v, rope_tabs, lamp, lng, *, layer, t_ctx, nb, ds, lam_init):
    p = cache_k.shape[2]
    tq = min(TQ_ATT, ds)
    sk = ds + p
    kc = 512 if sk % 512 == 0 else LANES
    assert t_ctx % ds == 0 and ds % tq == 0 and sk % kc == 0
    seq0, q0 = t_ctx // ds, t_ctx // tq
    nq = ds // tq
    qh, kh, vh = 0, Q_W // HEAD_W, 2 * Q_W // HEAD_W
    tab_q = pl.BlockSpec((tq, HEAD_W), lambda b, h, qi: (qi, 0))
    tab_k = pl.BlockSpec((ds, HEAD_W), lambda b, h, qi: (0, 0))
    cos, sa, sb = rope_tabs
    return pl.pallas_call(
        functools.partial(_attn_lat_body, lam_init=lam_init, kc=kc, rc=256),
        grid=(nb, A_HEADS, nq),
        in_specs=[
            pl.BlockSpec(lamp.shape, lambda b, h, qi: (0, 0)),
            pl.BlockSpec((1, V_DIM), lambda b, h, qi: (0, 0)),
            pl.BlockSpec((tq, HEAD_W), lambda b, h, qi: (q0 + b * nq + qi, qh + h)),
            pl.BlockSpec((ds, HEAD_W), lambda b, h, qi: (seq0 + b, kh + h)),
            pl.BlockSpec((ds, HEAD_W), lambda b, h, qi: (seq0 + b, vh + h)),
            pl.BlockSpec((1, 1, p, HEAD_W), lambda b, h, qi: (b, layer, 0, h)),
            pl.BlockSpec((1, 1, p, V_DIM), lambda b, h, qi: (b, layer, 0, h)),
            tab_q, tab_q, tab_q, tab_k, tab_k, tab_k,
        ],
        out_specs=pl.BlockSpec((tq, V_DIM), lambda b, h, qi: (b * nq + qi, h)),
        out_shape=jax.ShapeDtypeStruct((nb * ds, A_WIDTH), BF16),
        scratch_shapes=[
            pltpu.VMEM((sk, HEAD_W), BF16),
            pltpu.VMEM((sk, V_DIM + LANES), BF16),
            pltpu.VMEM((2, tq, sk), F32),
            pltpu.VMEM((2, tq, sk), BF16),
        ],
        compiler_params=_cparams(("parallel", "parallel", "arbitrary")),
        name="attn_lat",
    )(lamp, lng, z, z, z, cache_k, cache_v, cos, sa, sb, cos, sa, sb)


def _rope_tables(n_tok):
    n_rows = n_tok // GRID_W
    row = jnp.repeat(jnp.arange(n_rows), GRID_W).astype(F32)
    col = jnp.tile(jnp.arange(GRID_W), n_rows).astype(F32)
    inv = 1.0 / (ROPE_BASE ** (jnp.arange(N_FREQ, dtype=F32) / N_FREQ))
    ang = jnp.stack([row[:, None] * inv, col[:, None] * inv], axis=1)
    cos, sin = jnp.cos(ang), jnp.sin(ang)
    zero = jnp.zeros_like(sin)

    def lanes(first, second):
        per_axis = jnp.concatenate([first, second], axis=-1)
        return jnp.tile(per_axis.reshape(n_tok, 2 * 2 * N_FREQ), (1, 2))

    return lanes(cos, cos), lanes(-sin, zero), lanes(zero, sin)


def _pack_pair(hi, lo):
    hb = lax.bitcast_convert_type(hi.astype(BF16).astype(F32), U32)
    lb = lax.bitcast_convert_type(lo.astype(BF16).astype(F32), U32)
    return hb | (lb >> 16)


def _unpack_pair(u):
    hi = lax.bitcast_convert_type(u & jnp.uint32(0xFFFF0000), F32)
    lo = lax.bitcast_convert_type(u << 16, F32)
    return hi.astype(BF16), lo.astype(BF16)


def _hi_lo(x):
    hi = x.astype(BF16)
    return hi, (x - hi.astype(F32)).astype(BF16)


def _mixpost_body(oac_ref, oal_ref, zb_ref, zc_ref, zp_ref, zn_ref, xc_ref, xl_ref, mod_ref, g2_ref,
                  wout_ref, gng_ref, ws_ref, gb_ref, pw_ref, ps_ref, band_ref, bandp_ref, bandn_ref,
                  rw_ref, rb_ref, tri_ref, xo_ref, h_ref, idx_ref, gate_ref, rank_ref, cnt_ref, cat_scr, carry,
                  *, n_ctx_tiles, tps_ctx, tps_lat, n_exp):
    i = pl.program_id(0)
    is_ctx = i < n_ctx_tiles
    pos = jnp.where(is_ctx, i % tps_ctx, (i - n_ctx_tiles) % tps_lat)
    first = pos == 0
    last = pos == jnp.where(is_ctx, tps_ctx, tps_lat) - 1

    cat_scr[:, 0:A_WIDTH] = jnp.where(is_ctx, oac_ref[...], oal_ref[...])

    zb = jax.nn.gelu(zb_ref[...])
    u, v = zb[:, :B_WIDTH], zb[:, B_WIDTH:]
    for g in range(B_GROUPS):
        gs = slice(g * B_CH, (g + 1) * B_CH)
        vn = (_rms_lanes(v[:, gs]) * gng_ref[:, gs]).astype(BF16)
        for n in range(TILE // CHUNK):
            rs = slice(n * CHUNK, (n + 1) * CHUNK)
            mixed = _dot(ws_ref[g], vn[rs]) + gb_ref[:, gs]
            cat_scr[rs, A_WIDTH + g * B_CH:A_WIDTH + (g + 1) * B_CH] = (u[rs, gs] * mixed).astype(BF16)

    xc = zc_ref[...]
    xc_parts = _hi_lo(xc)
    xp_parts = _hi_lo(zp_ref[...] * jnp.where(first, 0.0, 1.0))
    xn_parts = _hi_lo(zn_ref[...] * jnp.where(last, 0.0, 1.0))
    r = lax.broadcasted_iota(I32, (TILE, 1), 0)
    for g, w in enumerate(POOL_WINDOWS):
        gs = slice(g * C_CH, (g + 1) * C_CH)
        half = w // 2
        acc = jnp.zeros((TILE, C_CH), F32)
        for part in xc_parts:
            acc = acc + _dot(band_ref[g], part[:, gs])
        for part in xp_parts:
            acc = acc + _dot(bandp_ref[g], part[:, gs])
        for part in xn_parts:
            acc = acc + _dot(bandn_ref[g], part[:, gs])
        left = jnp.where(first, jnp.minimum(half, r), half)
        right = jnp.where(last, jnp.minimum(half - 1, TILE - 1 - r), half - 1)
        cnt = (left + right + 1).astype(F32)
        pooled = acc / cnt - xc[:, gs]
        y = _dot(pooled.astype(BF16), pw_ref[g]) * ps_ref[:, gs]
        c0 = A_WIDTH + B_WIDTH + g * C_CH
        cat_scr[:, c0:c0 + C_CH] = y.astype(BF16)

    mix = _dot(cat_scr[...], wout_ref[...])
    x_new = jnp.where(is_ctx, xc_ref[...], xl_ref[...]) + mod_ref[0, 2:3, :] * mix
    xo_ref[...] = x_new
    h2 = _norm_mod(x_new, g2_ref[...], mod_ref[0, 3:4, :], mod_ref[0, 4:5, :])
    half_d = h2.shape[1] // 2
    h_ref[...] = _pack_pair(h2[:, :half_d], h2[:, half_d:])
    h_hi, h_lo = _hi_lo(h2)
    logits = _dot(h_hi, rw_ref[0]) + (_dot(h_lo, rw_ref[0]) + _dot(h_hi, rw_ref[1])) + rb_ref[...]
    _route_tile(logits, tri_ref, idx_ref, gate_ref, rank_ref, cnt_ref, carry, n_exp)


def _band_matrices():
    r = np.arange(TILE)[:, None]
    cur, prev, nxt = [], [], []
    for w in POOL_WINDOWS:
        half = w // 2
        c = np.arange(TILE)[None, :]
        cur.append((c >= r - half) & (c <= r + half - 1))
        ch = np.arange(HALO)[None, :]
        prev.append(ch - HALO >= r - half)
        nxt.append(TILE + ch <= r + half - 1)
    to = lambda m: jnp.asarray(np.stack(m).astype(np.float32), dtype=BF16)
    return to(cur), to(prev), to(nxt)


def _mixpost(oa_ctx, oa_lat, z, x_ctx, x_lat, mod, g2, w_out, gng, ws, gb, pw, ps, rw, rb, *, s_ctx, ds,
             n_exp):
    t_ctx, d = x_ctx.shape
    t = t_ctx + x_lat.shape[0]
    nt = t // TILE
    n_ctx_tiles = t_ctx // TILE
    tps_ctx, tps_lat = s_ctx // TILE, ds // TILE
    assert s_ctx % TILE == 0 and ds % TILE == 0
    grp = functools.partial(_group_of_block, rows=TILE, t_ctx=t_ctx, ds=ds)
    band, bandp, bandn = _band_matrices()
    zb_blk = (2 * Q_W + A_WIDTH) // (2 * B_WIDTH)
    zc_blk = (2 * Q_W + A_WIDTH + 2 * B_WIDTH) // C_WIDTH
    hpt = TILE // HALO
    const2 = lambda i: (0, 0)
    const3 = lambda i: (0, 0, 0)
    lane_blk = pl.BlockSpec((TILE, LANES), lambda i: (i, 0))
    tri = jnp.asarray(np.tril(np.ones((TILE, TILE), np.float32), -1), dtype=BF16)
    return pl.pallas_call(
        functools.partial(_mixpost_body, n_ctx_tiles=n_ctx_tiles, tps_ctx=tps_ctx, tps_lat=tps_lat,
                          n_exp=n_exp),
        grid=(nt,),
        in_specs=_pair_specs(TILE, A_WIDTH, n_ctx_tiles) + [
            pl.BlockSpec((TILE, 2 * B_WIDTH), lambda i: (i, zb_blk)),
            pl.BlockSpec((TILE, C_WIDTH), lambda i: (i, zc_blk)),
            pl.BlockSpec((HALO, C_WIDTH), lambda i: (jnp.maximum(i * hpt - 1, 0), zc_blk)),
            pl.BlockSpec((HALO, C_WIDTH), lambda i: (jnp.minimum((i + 1) * hpt, nt * hpt - 1), zc_blk)),
        ] + _pair_specs(TILE, d, n_ctx_tiles) + [
            pl.BlockSpec((1, 6, d), lambda i: (grp(i), 0, 0)),
            pl.BlockSpec((1, d), const2),
            pl.BlockSpec(w_out.shape, const2),
            pl.BlockSpec((1, B_WIDTH), const2),
            pl.BlockSpec(ws.shape, const3),
            pl.BlockSpec(gb.shape, const2),
            pl.BlockSpec(pw.shape, const3),
            pl.BlockSpec((1, C_WIDTH), const2),
            pl.BlockSpec(band.shape, const3),
            pl.BlockSpec(bandp.shape, const3),
            pl.BlockSpec(bandn.shape, const3),
            pl.BlockSpec(rw.shape, const3),
            pl.BlockSpec(rb.shape, const2),
            pl.BlockSpec((TILE, TILE), const2),
        ],
        out_specs=[
            pl.BlockSpec((TILE, d), lambda i: (i, 0)),
            pl.BlockSpec((TILE, d // 2), lambda i: (i, 0)),
            lane_blk, lane_blk, lane_blk,
            pl.BlockSpec((1, LANES), const2),
        ],
        out_shape=[
            jax.ShapeDtypeStruct((t, d), F32),
            jax.ShapeDtypeStruct((t, d // 2), U32),
            jax.ShapeDtypeStruct((t, LANES), I32),
            jax.ShapeDtypeStruct((t, LANES), F32),
            jax.ShapeDtypeStruct((t, LANES), I32),
            jax.ShapeDtypeStruct((1, LANES), F32),
        ],
        scratch_shapes=[pltpu.VMEM((TILE, d), BF16), pltpu.VMEM((1, LANES), F32)],
        compiler_params=_cparams(("arbitrary",)),
        name="mixpost",
    )(oa_ctx, oa_lat, z, z, z, z, x_ctx, x_lat, mod, g2, w_out, gng, ws, gb, pw, ps, band, bandp, bandn, rw, rb,
      tri)


def _route_tile(logits, tri_ref, idx_ref, gate_ref, rank_ref, cnt_ref, carry, n_exp):
    @pl.when(pl.program_id(0) == 0)
    def _():
        carry[...] = jnp.zeros_like(carry)

    lane = lax.broadcasted_iota(I32, logits.shape, 1).astype(F32)
    l = jnp.where(lane < n_exp, logits, -jnp.inf)
    vals, idxs, sels = [], [], []
    for _ in range(TOP_K):
        m = jnp.max(l, axis=-1, keepdims=True)
        idx = jnp.min(jnp.where(l == m, lane, float(LANES)), axis=-1, keepdims=True)
        sel = lane == idx
        vals.append(m)
        idxs.append(idx)
        sels.append(sel)
        l = jnp.where(sel, -jnp.inf, l)
    chosen = functools.reduce(jnp.logical_or, sels)
    chosen_f = jnp.where(chosen, 1.0, 0.0)
    prefix = carry[...] + _dot(tri_ref[...], chosen_f.astype(BF16))
    es = [jnp.exp(v - vals[0]) for v in vals]
    tot = functools.reduce(jnp.add, es)
    idx_o = jnp.zeros(logits.shape, F32)
    gate_o = jnp.zeros(logits.shape, F32)
    rank_o = jnp.zeros(logits.shape, F32)
    for k in range(TOP_K):
        rank_k = jnp.sum(jnp.where(sels[k], prefix, 0.0), axis=-1, keepdims=True)
        idx_o = jnp.where(lane == k, idxs[k], idx_o)
        gate_o = jnp.where(lane == k, es[k] / tot, gate_o)
        rank_o = jnp.where(lane == k, rank_k, rank_o)
    idx_ref[...] = idx_o.astype(I32)
    gate_ref[...] = gate_o
    rank_ref[...] = rank_o.astype(I32)
    carry[...] = carry[...] + jnp.sum(chosen_f, axis=0, keepdims=True)
    cnt_ref[...] = carry[...]


DMA_UNROLL = 8


def _dispatch_body(fill_ref, dest_ref, h_ref, xb_ref, zbuf, sem, zsem, *, n_exp):
    rows = h_ref.shape[0]

    @pl.when(pl.program_id(0) == 0)
    def _():
        zbuf[...] = jnp.zeros_like(zbuf)

        def fill_copy(e):
            start = pl.multiple_of(fill_ref[0, e], SUB_MOE)
            return pltpu.make_async_copy(zbuf, xb_ref.at[pl.ds(start, SUB_MOE)], zsem)

        for e in range(n_exp):
            pl.when(fill_ref[1, e] > 0)(lambda e=e: fill_copy(e).start())
        for e in range(n_exp):
            pl.when(fill_ref[1, e] > 0)(lambda e=e: fill_copy(e).wait())

    def row_copy(r, d):
        return pltpu.make_async_copy(h_ref.at[pl.ds(r, 1)], xb_ref.at[pl.ds(d, 1)], sem)

    def for_rows(fn):
        def body(g, carry):
            r0 = pl.multiple_of(g * DMA_UNROLL, DMA_UNROLL)
            for u in range(DMA_UNROLL):
                for k in range(TOP_K):
                    fn(row_copy(r0 + u, dest_ref[0, 0, (r0 + u) * TOP_K + k]))
            return carry

        lax.fori_loop(0, rows // DMA_UNROLL, body, 0)

    for_rows(lambda cp: cp.start())
    for_rows(lambda cp: cp.wait())


def _dispatch(fill, dest_tiles, h_packed, n_slots):
    t, w = h_packed.shape
    return pl.pallas_call(
        functools.partial(_dispatch_body, n_exp=fill.shape[1]),
        grid=(t // TILE,),
        in_specs=[
            pl.BlockSpec(memory_space=pltpu.SMEM),
            pl.BlockSpec((1, 1, TILE * TOP_K), lambda i: (i, 0, 0), memory_space=pltpu.SMEM),
            pl.BlockSpec((TILE, w), lambda i: (i, 0)),
        ],
        out_specs=pl.BlockSpec(memory_space=pl.ANY),
        out_shape=jax.ShapeDtypeStruct((n_slots, w), U32),
        scratch_shapes=[pltpu.VMEM((SUB_MOE, w), U32), pltpu.SemaphoreType.DMA(()), pltpu.SemaphoreType.DMA(())],
        compiler_params=_cparams(("arbitrary",)),
        name="dispatch",
    )(fill, dest_tiles, h_packed)


def _moe_body(be_ref, bv_ref, nu_ref, xb_ref, wg_ref, wu_ref, wd_ref, bg_ref, bu_ref, bd_ref,
              y_ref, x_scr):
    del be_ref, nu_ref
    i, j = pl.program_id(0), pl.program_id(1)
    valid = bv_ref[i]
    tm = xb_ref.shape[0]
    half_d = xb_ref.shape[1]
    n_sub = (valid + SUB_MOE - 1) // SUB_MOE

    @pl.when(valid > 0)
    def _():
        @pl.when(j == 0)
        def _():
            def unpack(s, carry):
                rs = pl.ds(pl.multiple_of(s * SUB_MOE, SUB_MOE), SUB_MOE)
                hi, lo = _unpack_pair(xb_ref[rs, :])
                x_scr[rs, 0:half_d] = hi
                x_scr[rs, half_d:] = lo
                return carry

            def init(s, carry):
                rs = pl.ds(pl.multiple_of(s * SUB_MOE, SUB_MOE), SUB_MOE)
                y_ref[rs, :] = jnp.broadcast_to(bd_ref[0, 0], (SUB_MOE, y_ref.shape[1]))
                return carry

            lax.fori_loop(0, n_sub, unpack, 0)
            lax.fori_loop(0, tm // SUB_MOE, init, 0)

        def ffn(rows):
            x = x_scr[0:rows, :]
            g = _dot(x, wg_ref[0, 0].astype(BF16)) + bg_ref[0, 0]
            u = _dot(x, wu_ref[0, 0].astype(BF16)) + bu_ref[0, 0]
            g = jnp.minimum(g, SWIGLU_LIMIT)
            u = jnp.clip(u, -SWIGLU_LIMIT, SWIGLU_LIMIT)
            a = g * jax.nn.sigmoid(SWIGLU_ALPHA * g) * (u + 1.0)
            y_ref[0:rows, :] += _dot(a.astype(BF16), wd_ref[0, 0].astype(BF16))

        for k in range(1, tm // SUB_MOE + 1):
            pl.when(n_sub == k)(functools.partial(ffn, k * SUB_MOE))


def _moe(blk_e, blk_valid, n_used, xb, w_gu, b_gu, w_dn, b_dn, *, layer):
    n_slots, half_d = xb.shape
    d = 2 * half_d
    n_exp, _, two_f = w_gu.shape[1:]
    f = two_f // 2
    tm, tf = TM_MOE, TF_MOE
    nf = f // tf
    nb = n_slots // tm

    def live(i, nu):
        return i < nu[0]

    def row_blk(i, j, be, bv, nu):
        return (jnp.minimum(i, nu[0] - 1), 0)

    def jj(i, j, nu):
        return jnp.where(live(i, nu), j, nf - 1)

    grid_spec = pltpu.PrefetchScalarGridSpec(
        num_scalar_prefetch=3,
        grid=(nb, nf),
        in_specs=[
            pl.BlockSpec((tm, half_d), row_blk),
            pl.BlockSpec((1, 1, d, tf), lambda i, j, be, bv, nu: (layer, be[i], 0, jj(i, j, nu))),
            pl.BlockSpec((1, 1, d, tf), lambda i, j, be, bv, nu: (layer, be[i], 0, nf + jj(i, j, nu))),
            pl.BlockSpec((1, 1, tf, d), lambda i, j, be, bv, nu: (layer, be[i], jj(i, j, nu), 0)),
            pl.BlockSpec((1, 1, 1, tf), lambda i, j, be, bv, nu: (layer, be[i], 0, jj(i, j, nu))),
            pl.BlockSpec((1, 1, 1, tf), lambda i, j, be, bv, nu: (layer, be[i], 0, nf + jj(i, j, nu))),
            pl.BlockSpec((1, 1, 1, d), lambda i, j, be, bv, nu: (layer, be[i], 0, 0)),
        ],
        out_specs=pl.BlockSpec((tm, d), row_blk),
        scratch_shapes=[pltpu.VMEM((tm, d), BF16)],
    )
    nl = w_gu.shape[0]
    return pl.pallas_call(
        _moe_body,
        grid_spec=grid_spec,
        out_shape=jax.ShapeDtypeStruct((n_slots, d), F32),
        compiler_params=_cparams(("arbitrary", "arbitrary")),
        name="moe",
    )(blk_e, blk_valid, n_used, xb, w_gu, w_gu, w_dn,
      b_gu.reshape(nl, n_exp, 1, two_f), b_gu.reshape(nl, n_exp, 1, two_f), b_dn.reshape(nl, n_exp, 1, d))


def _combine_body(dest_ref, destn_ref, gate_ref, x_ref, mod_ref, fg_ref, yb_ref, oc_ref, ol_ref, buf, sems,
                  *, final, n_ctx_tiles):
    rows = x_ref.shape[0]
    i = pl.program_id(0)
    cur = i % 2

    def row_copy(b, r, k, d):
        return pltpu.make_async_copy(yb_ref.at[pl.ds(d, 1)], buf.at[b, k, pl.ds(r, 1)], sems.at[b])

    def for_rows(b, d_ref, fn):
        def body(g, carry):
            r0 = pl.multiple_of(g * DMA_UNROLL, DMA_UNROLL)
            for u in range(DMA_UNROLL):
                for k in range(TOP_K):
                    fn(row_copy(b, r0 + u, k, d_ref[0, 0, (r0 + u) * TOP_K + k]))
            return carry

        lax.fori_loop(0, rows // DMA_UNROLL, body, 0)

    @pl.when(i == 0)
    def _():
        for_rows(0, dest_ref, lambda cp: cp.start())

    @pl.when(i + 1 < pl.num_programs(0))
    def _():
        for_rows(1 - cur, destn_ref, lambda cp: cp.start())

    for_rows(cur, dest_ref, lambda cp: cp.wait())

    ff = gate_ref[:, 0:1] * buf[cur, 0]
    for k in range(1, TOP_K):
        ff = ff + gate_ref[:, k:k + 1] * buf[cur, k]
    x_new = x_ref[...] + mod_ref[0, 5:6, :] * ff
    if final:
        x_new = _rms_lanes(x_new) * fg_ref[...]
    is_ctx = pl.program_id(0) < n_ctx_tiles

    @pl.when(is_ctx)
    def _():
        oc_ref[...] = x_new

    @pl.when(jnp.logical_not(is_ctx))
    def _():
        ol_ref[...] = x_new


def _combine(dest_tiles, gates, x, mod, final_g, yb, *, t_ctx, ds, final):
    t, d = x.shape
    n_ctx_tiles = t_ctx // TILE
    grp = functools.partial(_group_of_block, rows=TILE, t_ctx=t_ctx, ds=ds)
    nt = t // TILE
    return pl.pallas_call(
        functools.partial(_combine_body, final=final, n_ctx_tiles=n_ctx_tiles),
        grid=(nt,),
        in_specs=[
            pl.BlockSpec((1, 1, TILE * TOP_K), lambda i: (i, 0, 0), memory_space=pltpu.SMEM),
            pl.BlockSpec((1, 1, TILE * TOP_K), lambda i: (jnp.minimum(i + 1, nt - 1), 0, 0),
                         memory_space=pltpu.SMEM),
            pl.BlockSpec((TILE, LANES), lambda i: (i, 0)),
            pl.BlockSpec((TILE, d), lambda i: (i, 0)),
            pl.BlockSpec((1, 6, d), lambda i: (grp(i), 0, 0)),
            pl.BlockSpec((1, d), lambda i: (0, 0)),
            pl.BlockSpec(memory_space=pl.ANY),
        ],
        out_specs=_pair_specs(TILE, d, n_ctx_tiles),
        out_shape=[jax.ShapeDtypeStruct((t_ctx, d), F32), jax.ShapeDtypeStruct((t - t_ctx, d), F32)],
        scratch_shapes=[pltpu.VMEM((2, TOP_K, TILE, d), F32), pltpu.SemaphoreType.DMA((2,))],
        compiler_params=_cparams(("arbitrary",)),
        name="combine",
    )(dest_tiles, dest_tiles, gates, x, mod, final_g, yb)


def _slot_layout(idx, rank, counts, n_blocks):
    n_exp = counts.shape[0]
    n_blk = (counts + TM_MOE - 1) // TM_MOE
    padded = n_blk * TM_MOE
    pend = jnp.cumsum(padded)
    pstart = pend - padded
    per = jnp.maximum(-(-counts // jnp.maximum(n_blk, 1)), 1)
    per = (per + SUB_MOE - 1) // SUB_MOE * SUB_MOE
    per_tok = per[idx]
    blk_in = jnp.floor((rank.astype(F32) + 0.5) / per_tok.astype(F32)).astype(I32)
    dest = pstart[idx] + blk_in * TM_MOE + (rank - blk_in * per_tok)
    bstart = jnp.arange(n_blocks, dtype=I32) * TM_MOE
    n_used = (pend[-1] // TM_MOE).astype(I32).reshape(1)
    owner_of = jnp.minimum(bstart, jnp.maximum(pend[-1] - TM_MOE, 0))
    blk_e = jnp.minimum(jnp.searchsorted(pend, owner_of, side='right'), n_exp - 1).astype(I32)
    local = (bstart - pstart[blk_e]) // TM_MOE
    blk_valid = jnp.clip(counts[blk_e] - local * per[blk_e], 0, per[blk_e])
    blk_valid = jnp.where(bstart < pend[-1], blk_valid, 0).astype(I32)
    last_blk = jnp.maximum(counts - 1, 0) // per
    last_cnt = counts - last_blk * per
    last_group = pstart + last_blk * TM_MOE + jnp.maximum((last_cnt + SUB_MOE - 1) // SUB_MOE - 1, 0) * SUB_MOE
    fill = jnp.stack([last_group, (counts > 0).astype(I32)]).astype(I32)
    return dest.astype(I32), fill, blk_e, blk_valid, n_used


def kernel(x_prompt, x_sample, cache_k, cache_v, c, c_ctx, norm1_g, norm2_g, w_ada, b_ada, w_in, w_out,
           lam_q1, lam_k1, lam_q2, lam_k2, diff_ln_g, gmlp_norm_g, gmlp_ws, gmlp_bs, pool_w, pool_scale,
           router_w, router_b, exp_w_gu, exp_b_gu, exp_w_down, exp_b_down, final_g):
    nb_ctx, s_ctx, d = x_prompt.shape
    nb_lat, ds, _ = x_sample.shape
    depth = w_in.shape[0]
    n_exp = router_w.shape[-1]
    p_len = cache_k.shape[2]
    t_ctx, t_lat = nb_ctx * s_ctx, nb_lat * ds
    t = t_ctx + t_lat
    assert t_ctx % TILE == 0 and t_lat % TILE == 0 and (t * TOP_K) % TM_MOE == 0

    x_ctx, x_lat = x_prompt.reshape(t_ctx, d), x_sample.reshape(t_lat, d)
    cond_t = jnp.concatenate([c_ctx[None, :], c], axis=0).T
    mod = _ada(cond_t, w_ada, b_ada)
    mod = mod.reshape(depth, 1 + nb_lat, 6, d)

    rope_tabs = _rope_tables(ds)
    ck = cache_k.reshape(nb_lat, depth, p_len, A_HEADS * HEAD_W)
    cv = cache_v.reshape(nb_lat, depth, p_len, A_WIDTH)
    w_in_b = w_in.astype(BF16)
    w_out_b = w_out.astype(BF16)
    ws_b = gmlp_ws.astype(BF16)
    pw_b = pool_w.astype(BF16)
    rw_pad = jnp.pad(router_w, ((0, 0), (0, 0), (0, LANES - n_exp)))
    rw_hi = rw_pad.astype(BF16)
    rw_parts = jnp.stack([rw_hi, (rw_pad - rw_hi.astype(F32)).astype(BF16)], axis=1)
    rb_pad = jnp.pad(router_b, ((0, 0), (0, LANES - n_exp)))
    n_blocks = t * TOP_K // TM_MOE + n_exp

    caches = None
    for l in range(depth):
        lam_init = 0.8 - 0.6 * math.exp(-0.3 * l)
        lamp = jnp.stack([lam_q1[l], lam_k1[l], lam_q2[l], lam_k2[l]])
        lng = diff_ln_g[l][None, :]
        z = _inproj(x_ctx, x_lat, mod[l], norm1_g[l][None, :], w_in_b[l], ds=ds)
        oa_ctx, *caches = _attn_ctx(z, lamp, lng, caches, layer=l, depth=depth, nb=nb_ctx, s=s_ctx,
                                    lam_init=lam_init)
        oa_lat = _attn_lat(z, ck, cv, rope_tabs, lamp, lng, layer=l, t_ctx=t_ctx, nb=nb_lat, ds=ds,
                           lam_init=lam_init)
        gb = jnp.repeat(gmlp_bs[l].T, B_CH, axis=1)
        x_mid, h_packed, idx, gates, rank, counts = _mixpost(
            oa_ctx, oa_lat, z, x_ctx, x_lat, mod[l], norm2_g[l][None, :], w_out_b[l], gmlp_norm_g[l][None, :],
            ws_b[l], gb, pw_b[l], pool_scale[l][None, :], rw_parts[l], rb_pad[l][None, :], s_ctx=s_ctx, ds=ds,
            n_exp=n_exp)
        dest, fill, blk_e, blk_valid, n_used = _slot_layout(
            idx[:, :TOP_K], rank[:, :TOP_K], counts[0, :n_exp].astype(I32), n_blocks)
        dest_tiles = dest.reshape(t // TILE, 1, TILE * TOP_K)
        xb = _dispatch(fill, dest_tiles, h_packed, n_blocks * TM_MOE)
        yb = _moe(blk_e, blk_valid, n_used, xb, exp_w_gu, exp_b_gu, exp_w_down, exp_b_down, layer=l)
        x_ctx, x_lat = _combine(dest_tiles, gates, x_mid, mod[l], final_g[None, :], yb, t_ctx=t_ctx, ds=ds,
                                final=(l == depth - 1))

    return (x_ctx.reshape(nb_ctx, s_ctx, d), x_lat.reshape(nb_lat, ds, d),
            caches[0].reshape(nb_ctx, depth, s_ctx, A_HEADS, HEAD_W),
            caches[1].reshape(nb_ctx, depth, s_ctx, A_HEADS, V_DIM))
```

```python
import functools
import math

import numpy as np
import jax
import jax.numpy as jnp
from jax import lax
from jax.experimental import pallas as pl
from jax.experimental.pallas import tpu as pltpu

F32 = jnp.float32
BF16 = jnp.bfloat16
I32 = jnp.int32
U32 = jnp.uint32

GRID_W = 64
A_HEADS = 8
QK_DIM = 64
V_DIM = 2 * QK_DIM
HEAD_W = 2 * QK_DIM
Q_W = A_HEADS * HEAD_W
A_WIDTH = A_HEADS * V_DIM
B_GROUPS = 4
CHUNK = 128
B_CH = 128
B_WIDTH = B_GROUPS * B_CH
POOL_WINDOWS = (2, 4, 8, 16)
N_POOL = len(POOL_WINDOWS)
C_CH = 128
C_WIDTH = N_POOL * C_CH
ROPE_BASE = 10000.0
N_FREQ = QK_DIM // 4
TOP_K = 4
SWIGLU_ALPHA = 1.702
SWIGLU_LIMIT = 7.0
EPS = 1e-6
QK_SCALE = QK_DIM ** -0.5

LANES = 128
TILE = 256
HALO = 16
TM_PROJ_CHOICES = (1024, 512, 256)
LOG2E = 1.4426950408889634
TM_MOE = 1024
SUB_MOE = 64
TF_MOE = 256
TQ_ATT = 512
VMEM_LIMIT = 56 * 1024 * 1024


def _cparams(sem, vmem=VMEM_LIMIT):
    return pltpu.CompilerParams(dimension_semantics=sem, vmem_limit_bytes=vmem)


def _norm_mod(x, g, shift, scale):
    ms = jnp.mean(x * x, axis=-1, keepdims=True)
    return x * lax.rsqrt(ms + EPS) * g * (1.0 + scale) + shift


def _rms_lanes(x):
    return x * lax.rsqrt(jnp.mean(x * x, axis=-1, keepdims=True) + EPS)


def _dot(a, b):
    return jnp.dot(a, b, preferred_element_type=F32)


def _dot_nt(a, b):
    return lax.dot_general(a, b, (((1,), (1,)), ((), ())), preferred_element_type=F32)


def _ada_body(ct_ref, w_ref, b_ref, o_ref, sb, *, kc):
    d, ng = ct_ref.shape
    tn = w_ref.shape[-1]

    @pl.when(jnp.logical_and(pl.program_id(0) == 0, pl.program_id(1) == 0))
    def _():
        def fill(c, carry):
            k0 = pl.multiple_of(c * kc, kc)
            s = jax.nn.silu(ct_ref[pl.ds(k0, kc), :])
            for gi in range(ng):
                sb[gi, pl.ds(k0, kc), :] = jnp.broadcast_to(s[:, gi:gi + 1], (kc, LANES))
            return carry

        lax.fori_loop(0, d // kc, fill, 0)

    def body(c, accs):
        k0 = pl.multiple_of(c * kc, kc)
        w = w_ref[0, pl.ds(k0, kc), :]
        out = []
        for gi, acc in enumerate(accs):
            s = sb[gi, pl.ds(k0, kc), :]
            parts = [jnp.sum((w[:, t * LANES:(t + 1) * LANES] * s).reshape(kc // 8, 8, LANES), axis=0)
                     for t in range(tn // LANES)]
            out.append(acc + jnp.concatenate(parts, axis=1))
        return tuple(out)

    accs = lax.fori_loop(0, d // kc, body, tuple(jnp.zeros((8, tn), F32) for _ in range(ng)))
    for gi in range(ng):
        o_ref[0, gi:gi + 1, :] = jnp.sum(accs[gi], axis=0, keepdims=True) + b_ref[0]


def _ada(cond_t, w_ada, b_ada):
    d, ng = cond_t.shape
    nl, _, n = w_ada.shape
    tn = 1024
    return pl.pallas_call(
        functools.partial(_ada_body, kc=32),
        grid=(nl, n // tn),
        in_specs=[
            pl.BlockSpec((d, ng), lambda l, j: (0, 0)),
            pl.BlockSpec((1, d, tn), lambda l, j: (l, 0, j)),
            pl.BlockSpec((1, 1, tn), lambda l, j: (l, 0, j)),
        ],
        out_specs=pl.BlockSpec((1, ng, tn), lambda l, j: (l, 0, j)),
        out_shape=jax.ShapeDtypeStruct((nl, ng, n), F32),
        scratch_shapes=[pltpu.VMEM((ng, d, LANES), F32)],
        compiler_params=_cparams(("arbitrary", "arbitrary")),
        name="ada",
    )(cond_t, w_ada, b_ada.reshape(nl, 1, n))


def _inproj_body(xc_ref, xl_ref, mod_ref, g_ref, w_ref, z_ref, h_scr, *, rc, n_ctx_blocks):
    @pl.when(pl.program_id(1) == 0)
    def _():
        shift = mod_ref[0, 0:1, :]
        scale = mod_ref[0, 1:2, :]
        g = g_ref[...]

        def fill(x_ref):
            def body(c, carry):
                r0 = pl.multiple_of(c * rc, rc)
                h = _norm_mod(x_ref[pl.ds(r0, rc), :], g, shift, scale)
                h_scr[pl.ds(r0, rc), :] = h.astype(BF16)
                return carry

            lax.fori_loop(0, x_ref.shape[0] // rc, body, 0)

        is_ctx = pl.program_id(0) < n_ctx_blocks
        pl.when(is_ctx)(lambda: fill(xc_ref))
        pl.when(jnp.logical_not(is_ctx))(lambda: fill(xl_ref))

    z_ref[...] = _dot(h_scr[...], w_ref[...])


def _group_of_block(i, rows, t_ctx, ds):
    r = i * rows
    return jnp.where(r < t_ctx, 0, 1 + (r - t_ctx) // ds)


def _pair_specs(rows, width, n_ctx_blocks):
    return [
        pl.BlockSpec((rows, width), lambda i, *_: (jnp.minimum(i, n_ctx_blocks - 1), 0)),
        pl.BlockSpec((rows, width), lambda i, *_: (jnp.maximum(i - n_ctx_blocks, 0), 0)),
    ]


def _inproj(x_ctx, x_lat, mod, g1, w_in, *, ds):
    t_ctx, d = x_ctx.shape
    t = t_ctx + x_lat.shape[0]
    n = w_in.shape[1]
    tm = next(m for m in TM_PROJ_CHOICES if t_ctx % m == 0 and ds % m == 0)
    tn = 768
    assert n % tn == 0
    grp = functools.partial(_group_of_block, rows=tm, t_ctx=t_ctx, ds=ds)
    return pl.pallas_call(
        functools.partial(_inproj_body, rc=64, n_ctx_blocks=t_ctx // tm),
        grid=(t // tm, n // tn),
        in_specs=_pair_specs(tm, d, t_ctx // tm) + [
            pl.BlockSpec((1, 6, d), lambda i, j: (grp(i), 0, 0)),
            pl.BlockSpec((1, d), lambda i, j: (0, 0)),
            pl.BlockSpec((d, tn), lambda i, j: (0, j)),
        ],
        out_specs=pl.BlockSpec((tm, tn), lambda i, j: (i, j)),
        out_shape=jax.ShapeDtypeStruct((t, n), F32),
        scratch_shapes=[pltpu.VMEM((tm, d), BF16)],
        compiler_params=_cparams(("arbitrary", "arbitrary")),
        name="inproj",
    )(x_ctx, x_lat, mod, g1, w_in)


def _lam_value(lamp_ref, lam_init):
    lp = lamp_ref[...]
    a = jnp.sum(lp[0:1] * lp[1:2], keepdims=True)
    b = jnp.sum(lp[2:3] * lp[3:4], keepdims=True)
    return jnp.exp(a) - jnp.exp(b) + lam_init


def _split_halves(q):
    first = lax.broadcasted_iota(I32, (1, HEAD_W), 1) < QK_DIM
    return jnp.where(first, q, 0.0).astype(BF16), jnp.where(first, 0.0, q).astype(BF16)


def _attn_ctx_body(lamp_ref, lng_ref, q_ref, k_ref, v_ref, *rest, lam_init):
    o_ref, ko_ref, vo_ref = rest[-3:]
    lam = _lam_value(lamp_ref, lam_init)
    g = lng_ref[...] * (1.0 - lam_init)
    ko_ref[0, 0] = k_ref[...]
    vo_ref[0, 0] = v_ref[...]
    for h in range(A_HEADS):
        hs = slice(h * HEAD_W, (h + 1) * HEAD_W)
        k = k_ref[:, hs].astype(BF16)
        v = v_ref[:, hs].astype(BF16)
        ps = []
        for qh in _split_halves(q_ref[:, hs] * QK_SCALE):
            s = _dot_nt(qh, k)
            e = jnp.exp(s - jnp.max(s, axis=-1, keepdims=True))
            ps.append(e * (1.0 / jnp.sum(e, axis=-1, keepdims=True)))
        o = _dot((ps[0] - lam * ps[1]).astype(BF16), v)
        o_ref[:, hs] = (_rms_lanes(o) * g).astype(o_ref.dtype)


def _attn_ctx(z, lamp, lng, caches, *, layer, depth, nb, s, lam_init):
    cache_blk = pl.BlockSpec((1, 1, s, Q_W), lambda b: (b, layer, 0, 0))
    n_in = 5
    return pl.pallas_call(
        functools.partial(_attn_ctx_body, lam_init=lam_init),
        grid=(nb,),
        in_specs=[
            pl.BlockSpec(lamp.shape, lambda b: (0, 0)),
            pl.BlockSpec((1, V_DIM), lambda b: (0, 0)),
            pl.BlockSpec((s, Q_W), lambda b: (b, 0)),
            pl.BlockSpec((s, Q_W), lambda b: (b, 1)),
            pl.BlockSpec((s, A_WIDTH), lambda b: (b, 2)),
        ] + ([] if caches is None else [pl.BlockSpec(memory_space=pl.ANY)] * 2),
        out_specs=[pl.BlockSpec((s, A_WIDTH), lambda b: (b, 0)), cache_blk, cache_blk],
        out_shape=[
            jax.ShapeDtypeStruct((nb * s, A_WIDTH), BF16),
            jax.ShapeDtypeStruct((nb, depth, s, Q_W), F32),
            jax.ShapeDtypeStruct((nb, depth, s, A_WIDTH), F32),
        ],
        input_output_aliases={} if caches is None else {n_in: 1, n_in + 1: 2},
        compiler_params=_cparams(("arbitrary",)),
        name="attn_ctx",
    )(lamp, lng, z, z, z, *(() if caches is None else caches))


def _rope(x, c, sa, sb):
    return x * c + pltpu.roll(x, HEAD_W - N_FREQ, 1) * sa + pltpu.roll(x, N_FREQ, 1) * sb


def _attn_lat_body(lamp_ref, lng_ref, q_ref, k_ref, v_ref, ck_ref, cv_ref,
                   cq_ref, saq_ref, sbq_ref, ckk_ref, sak_ref, sbk_ref,
                   o_ref, kall, vall, s_scr, e_scr, *, lam_init, kc, rc):
    ds = k_ref.shape[0]
    p = ck_ref.shape[2]
    tq = q_ref.shape[0]
    nch = (ds + p) // kc

    @pl.when(pl.program_id(2) == 0)
    def _():
        def ones_col(rows):
            return jnp.where(lax.broadcasted_iota(I32, (rows, LANES), 1) == 0, 1.0, 0.0).astype(BF16)

        def body(c, carry):
            r0 = pl.multiple_of(c * rc, rc)
            rs = pl.ds(r0, rc)
            kall[rs, :] = _rope(k_ref[rs, :], ckk_ref[rs, :], sak_ref[rs, :], sbk_ref[rs, :]).astype(BF16)
            vall[rs, 0:V_DIM] = v_ref[rs, :].astype(BF16)
            vall[rs, V_DIM:] = ones_col(rc)
            return carry

        lax.fori_loop(0, ds // rc, body, 0)
        kall[ds:ds + p, :] = ck_ref[0, 0].astype(BF16)
        vall[ds:ds + p, 0:V_DIM] = cv_ref[0, 0].astype(BF16)
        vall[ds:ds + p, V_DIM:] = ones_col(p)

    lam = _lam_value(lamp_ref, lam_init)
    q = _rope(q_ref[...], cq_ref[...], saq_ref[...], sbq_ref[...]) * (QK_SCALE * LOG2E)
    halves = _split_halves(q)
    chunks = [slice(c * kc, (c + 1) * kc) for c in range(nch)]

    def qk(n, ch, mrun):
        s = _dot_nt(halves[n], kall[ch, :])
        s_scr[n, :, ch] = s
        for t in range(kc // LANES):
            mrun = jnp.maximum(mrun, s[:, t * LANES:(t + 1) * LANES])
        return mrun

    def ex(n, ch, m):
        e_scr[n, :, ch] = jnp.exp2(s_scr[n, :, ch] - m).astype(BF16)

    neg = jnp.full((tq, LANES), -jnp.inf, F32)
    mrun = neg
    for ch in chunks:
        mrun = qk(0, ch, mrun)
    m0 = jnp.max(mrun, axis=-1, keepdims=True)
    mrun = neg
    for ch in chunks:
        mrun = qk(1, ch, mrun)
        ex(0, ch, m0)
    m1 = jnp.max(mrun, axis=-1, keepdims=True)
    oe0 = jnp.zeros((tq, V_DIM + LANES), F32)
    for ch in chunks:
        ex(1, ch, m1)
        oe0 = oe0 + _dot(e_scr[0, :, ch], vall[ch, :])
    oe1 = _dot(e_scr[1], vall[...])
    outs = [oe[:, 0:V_DIM] * (1.0 / oe[:, V_DIM:V_DIM + 1]) for oe in (oe0, oe1)]
    o = outs[0] - lam * outs[1]
    o_ref[...] = (_rms_lanes(o) * (lng_ref[...] * (1.0 - lam_init))).astype(o_ref.dtype)


def _attn_lat(z, cache_k, cache_v, rope_tabs, lamp, lng, *, layer, t_ctx, nb, ds, lam_init):
    p = cache_k.shape[2]
    tq = min(TQ_ATT, ds)
    sk = ds + p
    kc = 512 if sk % 512 == 0 else LANES
    assert t_ctx % ds == 0 and ds % tq == 0 and sk % kc == 0
    seq0, q0 = t_ctx // ds, t_ctx // tq
    nq = ds // tq
    qh, kh, vh = 0, Q_W // HEAD_W, 2 * Q_W // HEAD_W
    tab_q = pl.BlockSpec((tq, HEAD_W), lambda b, h, qi: (qi, 0))
    tab_k = pl.BlockSpec((ds, HEAD_W), lambda b, h, qi: (0, 0))
    cos, sa, sb = rope_tabs
    return pl.pallas_call(
        functools.partial(_attn_lat_body, lam_init=lam_init, kc=kc, rc=256),
        grid=(nb, A_HEADS, nq),
        in_specs=[
            pl.BlockSpec(lamp.shape, lambda b, h, qi: (0, 0)),
            pl.BlockSpec((1, V_DIM), lambda b, h, qi: (0, 0)),
            pl.BlockSpec((tq, HEAD_W), lambda b, h, qi: (q0 + b * nq + qi, qh + h)),
            pl.BlockSpec((ds, HEAD_W), lambda b, h, qi: (seq0 + b, kh + h)),
            pl.BlockSpec((ds, HEAD_W), lambda b, h, qi: (seq0 + b, vh + h)),
            pl.BlockSpec((1, 1, p, HEAD_W), lambda b, h, qi: (b, layer, 0, h)),
            pl.BlockSpec((1, 1, p, V_DIM), lambda b, h, qi: (b, layer, 0, h)),
            tab_q, tab_q, tab_q, tab_k, tab_k, tab_k,
        ],
        out_specs=pl.BlockSpec((tq, V_DIM), lambda b, h, qi: (b * nq + qi, h)),
        out_shape=jax.ShapeDtypeStruct((nb * ds, A_WIDTH), BF16),
        scratch_shapes=[
            pltpu.VMEM((sk, HEAD_W), BF16),
            pltpu.VMEM((sk, V_DIM + LANES), BF16),
            pltpu.VMEM((2, tq, sk), F32),
            pltpu.VMEM((2, tq, sk), BF16),
        ],
        compiler_params=_cparams(("parallel", "parallel", "arbitrary")),
        name="attn_lat",
    )(lamp, lng, z, z, z, cache_k, cache_v, cos, sa, sb, cos, sa, sb)


def _rope_tables(n_tok):
    n_rows = n_tok // GRID_W
    row = jnp.repeat(jnp.arange(n_rows), GRID_W).astype(F32)
    col = jnp.tile(jnp.arange(GRID_W), n_rows).astype(F32)
    inv = 1.0 / (ROPE_BASE ** (jnp.arange(N_FREQ, dtype=F32) / N_FREQ))
    ang = jnp.stack([row[:, None] * inv, col[:, None] * inv], axis=1)
    cos, sin = jnp.cos(ang), jnp.sin(ang)
    zero = jnp.zeros_like(sin)

    def lanes(first, second):
        per_axis = jnp.concatenate([first, second], axis=-1)
        return jnp.tile(per_axis.reshape(n_tok, 2 * 2 * N_FREQ), (1, 2))

    return lanes(cos, cos), lanes(-sin, zero), lanes(zero, sin)


def _pack_pair(hi, lo):
    hb = lax.bitcast_convert_type(hi.astype(BF16).astype(F32), U32)
    lb = lax.bitcast_convert_type(lo.astype(BF16).astype(F32), U32)
    return hb | (lb >> 16)


def _unpack_pair(u):
    hi = lax.bitcast_convert_type(u & jnp.uint32(0xFFFF0000), F32)
    lo = lax.bitcast_convert_type(u << 16, F32)
    return hi.astype(BF16), lo.astype(BF16)


def _hi_lo(x):
    hi = x.astype(BF16)
    return hi, (x - hi.astype(F32)).astype(BF16)


def _mixpost_body(oac_ref, oal_ref, zb_ref, zc_ref, zp_ref, zn_ref, xc_ref, xl_ref, mod_ref, g2_ref,
                  wout_ref, gng_ref, ws_ref, gb_ref, pw_ref, ps_ref, band_ref, bandp_ref, bandn_ref,
                  rw_ref, rb_ref, tri_ref, xo_ref, h_ref, idx_ref, gate_ref, rank_ref, cnt_ref, cat_scr, carry,
                  *, n_ctx_tiles, tps_ctx, tps_lat, n_exp):
    i = pl.program_id(0)
    is_ctx = i < n_ctx_tiles
    pos = jnp.where(is_ctx, i % tps_ctx, (i - n_ctx_tiles) % tps_lat)
    first = pos == 0
    last = pos == jnp.where(is_ctx, tps_ctx, tps_lat) - 1

    cat_scr[:, 0:A_WIDTH] = jnp.where(is_ctx, oac_ref[...], oal_ref[...])

    zb = jax.nn.gelu(zb_ref[...])
    u, v = zb[:, :B_WIDTH], zb[:, B_WIDTH:]
    for g in range(B_GROUPS):
        gs = slice(g * B_CH, (g + 1) * B_CH)
        vn = (_rms_lanes(v[:, gs]) * gng_ref[:, gs]).astype(BF16)
        for n in range(TILE // CHUNK):
            rs = slice(n * CHUNK, (n + 1) * CHUNK)
            mixed = _dot(ws_ref[g], vn[rs]) + gb_ref[:, gs]
            cat_scr[rs, A_WIDTH + g * B_CH:A_WIDTH + (g + 1) * B_CH] = (u[rs, gs] * mixed).astype(BF16)

    xc = zc_ref[...]
    xc_parts = _hi_lo(xc)
    xp_parts = _hi_lo(zp_ref[...] * jnp.where(first, 0.0, 1.0))
    xn_parts = _hi_lo(zn_ref[...] * jnp.where(last, 0.0, 1.0))
    r = lax.broadcasted_iota(I32, (TILE, 1), 0)
    for g, w in enumerate(POOL_WINDOWS):
        gs = slice(g * C_CH, (g + 1) * C_CH)
        half = w // 2
        acc = jnp.zeros((TILE, C_CH), F32)
        for part in xc_parts:
            acc = acc + _dot(band_ref[g], part[:, gs])
        for part in xp_parts:
            acc = acc + _dot(bandp_ref[g], part[:, gs])
        for part in xn_parts:
            acc = acc + _dot(bandn_ref[g], part[:, gs])
        left = jnp.where(first, jnp.minimum(half, r), half)
        right = jnp.where(last, jnp.minimum(half - 1, TILE - 1 - r), half - 1)
        cnt = (left + right + 1).astype(F32)
        pooled = acc / cnt - xc[:, gs]
        y = _dot(pooled.astype(BF16), pw_ref[g]) * ps_ref[:, gs]
        c0 = A_WIDTH + B_WIDTH + g * C_CH
        cat_scr[:, c0:c0 + C_CH] = y.astype(BF16)

    mix = _dot(cat_scr[...], wout_ref[...])
    x_new = jnp.where(is_ctx, xc_ref[...], xl_ref[...]) + mod_ref[0, 2:3, :] * mix
    xo_ref[...] = x_new
    h2 = _norm_mod(x_new, g2_ref[...], mod_ref[0, 3:4, :], mod_ref[0, 4:5, :])
    half_d = h2.shape[1] // 2
    h_ref[...] = _pack_pair(h2[:, :half_d], h2[:, half_d:])
    h_hi, h_lo = _hi_lo(h2)
    logits = _dot(h_hi, rw_ref[0]) + (_dot(h_lo, rw_ref[0]) + _dot(h_hi, rw_ref[1])) + rb_ref[...]
    _route_tile(logits, tri_ref, idx_ref, gate_ref, rank_ref, cnt_ref, carry, n_exp)


def _band_matrices():
    r = np.arange(TILE)[:, None]
    cur, prev, nxt = [], [], []
    for w in POOL_WINDOWS:
        half = w // 2
        c = np.arange(TILE)[None, :]
        cur.append((c >= r - half) & (c <= r + half - 1))
        ch = np.arange(HALO)[None, :]
        prev.append(ch - HALO >= r - half)
        nxt.append(TILE + ch <= r + half - 1)
    to = lambda m: jnp.asarray(np.stack(m).astype(np.float32), dtype=BF16)
    return to(cur), to(prev), to(nxt)


def _mixpost(oa_ctx, oa_lat, z, x_ctx, x_lat, mod, g2, w_out, gng, ws, gb, pw, ps, rw, rb, *, s_ctx, ds,
             n_exp):
    t_ctx, d = x_ctx.shape
    t = t_ctx + x_lat.shape[0]
    nt = t // TILE
    n_ctx_tiles = t_ctx // TILE
    tps_ctx, tps_lat = s_ctx // TILE, ds // TILE
    assert s_ctx % TILE == 0 and ds % TILE == 0
    grp = functools.partial(_group_of_block, rows=TILE, t_ctx=t_ctx, ds=ds)
    band, bandp, bandn = _band_matrices()
    zb_blk = (2 * Q_W + A_WIDTH) // (2 * B_WIDTH)
    zc_blk = (2 * Q_W + A_WIDTH + 2 * B_WIDTH) // C_WIDTH
    hpt = TILE // HALO
    const2 = lambda i: (0, 0)
    const3 = lambda i: (0, 0, 0)
    lane_blk = pl.BlockSpec((TILE, LANES), lambda i: (i, 0))
    tri = jnp.asarray(np.tril(np.ones((TILE, TILE), np.float32), -1), dtype=BF16)
    return pl.pallas_call(
        functools.partial(_mixpost_body, n_ctx_tiles=n_ctx_tiles, tps_ctx=tps_ctx, tps_lat=tps_lat,
                          n_exp=n_exp),
        grid=(nt,),
        in_specs=_pair_specs(TILE, A_WIDTH, n_ctx_tiles) + [
            pl.BlockSpec((TILE, 2 * B_WIDTH), lambda i: (i, zb_blk)),
            pl.BlockSpec((TILE, C_WIDTH), lambda i: (i, zc_blk)),
            pl.BlockSpec((HALO, C_WIDTH), lambda i: (jnp.maximum(i * hpt - 1, 0), zc_blk)),
            pl.BlockSpec((HALO, C_WIDTH), lambda i: (jnp.minimum((i + 1) * hpt, nt * hpt - 1), zc_blk)),
        ] + _pair_specs(TILE, d, n_ctx_tiles) + [
            pl.BlockSpec((1, 6, d), lambda i: (grp(i), 0, 0)),
            pl.BlockSpec((1, d), const2),
            pl.BlockSpec(w_out.shape, const2),
            pl.BlockSpec((1, B_WIDTH), const2),
            pl.BlockSpec(ws.shape, const3),
            pl.BlockSpec(gb.shape, const2),
            pl.BlockSpec(pw.shape, const3),
            pl.BlockSpec((1, C_WIDTH), const2),
            pl.BlockSpec(band.shape, const3),
            pl.BlockSpec(bandp.shape, const3),
            pl.BlockSpec(bandn.shape, const3),
            pl.BlockSpec(rw.shape, const3),
            pl.BlockSpec(rb.shape, const2),
            pl.BlockSpec((TILE, TILE), const2),
        ],
        out_specs=[
            pl.BlockSpec((TILE, d), lambda i: (i, 0)),
            pl.BlockSpec((TILE, d // 2), lambda i: (i, 0)),
            lane_blk, lane_blk, lane_blk,
            pl.BlockSpec((1, LANES), const2),
        ],
        out_shape=[
            jax.ShapeDtypeStruct((t, d), F32),
            jax.ShapeDtypeStruct((t, d // 2), U32),
            jax.ShapeDtypeStruct((t, LANES), I32),
            jax.ShapeDtypeStruct((t, LANES), F32),
            jax.ShapeDtypeStruct((t, LANES), I32),
            jax.ShapeDtypeStruct((1, LANES), F32),
        ],
        scratch_shapes=[pltpu.VMEM((TILE, d), BF16), pltpu.VMEM((1, LANES), F32)],
        compiler_params=_cparams(("arbitrary",)),
        name="mixpost",
    )(oa_ctx, oa_lat, z, z, z, z, x_ctx, x_lat, mod, g2, w_out, gng, ws, gb, pw, ps, band, bandp, bandn, rw, rb,
      tri)


def _route_tile(logits, tri_ref, idx_ref, gate_ref, rank_ref, cnt_ref, carry, n_exp):
    @pl.when(pl.program_id(0) == 0)
    def _():
        carry[...] = jnp.zeros_like(carry)

    lane = lax.broadcasted_iota(I32, logits.shape, 1).astype(F32)
    l = jnp.where(lane < n_exp, logits, -jnp.inf)
    vals, idxs, sels = [], [], []
    for _ in range(TOP_K):
        m = jnp.max(l, axis=-1, keepdims=True)
        idx = jnp.min(jnp.where(l == m, lane, float(LANES)), axis=-1, keepdims=True)
        sel = lane == idx
        vals.append(m)
        idxs.append(idx)
        sels.append(sel)
        l = jnp.where(sel, -jnp.inf, l)
    chosen = functools.reduce(jnp.logical_or, sels)
    chosen_f = jnp.where(chosen, 1.0, 0.0)
    prefix = carry[...] + _dot(tri_ref[...], chosen_f.astype(BF16))
    es = [jnp.exp(v - vals[0]) for v in vals]
    tot = functools.reduce(jnp.add, es)
    idx_o = jnp.zeros(logits.shape, F32)
    gate_o = jnp.zeros(logits.shape, F32)
    rank_o = jnp.zeros(logits.shape, F32)
    for k in range(TOP_K):
        rank_k = jnp.sum(jnp.where(sels[k], prefix, 0.0), axis=-1, keepdims=True)
        idx_o = jnp.where(lane == k, idxs[k], idx_o)
        gate_o = jnp.where(lane == k, es[k] / tot, gate_o)
        rank_o = jnp.where(lane == k, rank_k, rank_o)
    idx_ref[...] = idx_o.astype(I32)
    gate_ref[...] = gate_o
    rank_ref[...] = rank_o.astype(I32)
    carry[...] = carry[...] + jnp.sum(chosen_f, axis=0, keepdims=True)
    cnt_ref[...] = carry[...]


DMA_UNROLL = 8


def _dispatch_body(fill_ref, dest_ref, h_ref, xb_ref, zbuf, sem, zsem, *, n_exp):
    rows = h_ref.shape[0]

    @pl.when(pl.program_id(0) == 0)
    def _():
        zbuf[...] = jnp.zeros_like(zbuf)

        def fill_copy(e):
            start = pl.multiple_of(fill_ref[0, e], SUB_MOE)
            return pltpu.make_async_copy(zbuf, xb_ref.at[pl.ds(start, SUB_MOE)], zsem)

        for e in range(n_exp):
            pl.when(fill_ref[1, e] > 0)(lambda e=e: fill_copy(e).start())
        for e in range(n_exp):
            pl.when(fill_ref[1, e] > 0)(lambda e=e: fill_copy(e).wait())

    def row_copy(r, d):
        return pltpu.make_async_copy(h_ref.at[pl.ds(r, 1)], xb_ref.at[pl.ds(d, 1)], sem)

    def for_rows(fn):
        def body(g, carry):
            r0 = pl.multiple_of(g * DMA_UNROLL, DMA_UNROLL)
            for u in range(DMA_UNROLL):
                for k in range(TOP_K):
                    fn(row_copy(r0 + u, dest_ref[0, 0, (r0 + u) * TOP_K + k]), k)
            return carry

        lax.fori_loop(0, rows // DMA_UNROLL, body, 0)

    for_rows(lambda cp, n: cp.start(priority=n % 2))
    for_rows(lambda cp, n: cp.wait())


def _dispatch(fill, dest_tiles, h_packed, n_slots):
    t, w = h_packed.shape
    return pl.pallas_call(
        functools.partial(_dispatch_body, n_exp=fill.shape[1]),
        grid=(t // TILE,),
        in_specs=[
            pl.BlockSpec(memory_space=pltpu.SMEM),
            pl.BlockSpec((1, 1, TILE * TOP_K), lambda i: (i, 0, 0), memory_space=pltpu.SMEM),
            pl.BlockSpec((TILE, w), lambda i: (i, 0)),
        ],
        out_specs=pl.BlockSpec(memory_space=pl.ANY),
        out_shape=jax.ShapeDtypeStruct((n_slots, w), U32),
        scratch_shapes=[pltpu.VMEM((SUB_MOE, w), U32), pltpu.SemaphoreType.DMA(()), pltpu.SemaphoreType.DMA(())],
        compiler_params=_cparams(("arbitrary",)),
        name="dispatch",
    )(fill, dest_tiles, h_packed)


def _moe_body(be_ref, bv_ref, nu_ref, xb_ref, wg_ref, wu_ref, wd_ref, bg_ref, bu_ref, bd_ref,
              y_ref, x_scr):
    del be_ref, nu_ref
    i, j = pl.program_id(0), pl.program_id(1)
    valid = bv_ref[i]
    tm = xb_ref.shape[0]
    half_d = xb_ref.shape[1]
    n_sub = (valid + SUB_MOE - 1) // SUB_MOE

    @pl.when(valid > 0)
    def _():
        @pl.when(j == 0)
        def _():
            def unpack(s, carry):
                rs = pl.ds(pl.multiple_of(s * SUB_MOE, SUB_MOE), SUB_MOE)
                hi, lo = _unpack_pair(xb_ref[rs, :])
                x_scr[rs, 0:half_d] = hi
                x_scr[rs, half_d:] = lo
                return carry

            def init(s, carry):
                rs = pl.ds(pl.multiple_of(s * SUB_MOE, SUB_MOE), SUB_MOE)
                y_ref[rs, :] = jnp.broadcast_to(bd_ref[0, 0], (SUB_MOE, y_ref.shape[1]))
                return carry

            lax.fori_loop(0, n_sub, unpack, 0)
            lax.fori_loop(0, tm // SUB_MOE, init, 0)

        def ffn(rows):
            x = x_scr[0:rows, :]
            g = _dot(x, wg_ref[0, 0].astype(BF16)) + bg_ref[0, 0]
            u = _dot(x, wu_ref[0, 0].astype(BF16)) + bu_ref[0, 0]
            g = jnp.minimum(g, SWIGLU_LIMIT)
            u = jnp.clip(u, -SWIGLU_LIMIT, SWIGLU_LIMIT)
            a = g * jax.nn.sigmoid(SWIGLU_ALPHA * g) * (u + 1.0)
            y_ref[0:rows, :] += _dot(a.astype(BF16), wd_ref[0, 0].astype(BF16))

        for k in range(1, tm // SUB_MOE + 1):
            pl.when(n_sub == k)(functools.partial(ffn, k * SUB_MOE))


def _moe(blk_e, blk_valid, n_used, xb, w_gu, b_gu, w_dn, b_dn, *, layer):
    n_slots, half_d = xb.shape
    d = 2 * half_d
    n_exp, _, two_f = w_gu.shape[1:]
    f = two_f // 2
    tm, tf = TM_MOE, TF_MOE
    nf = f // tf
    nb = n_slots // tm

    def live(i, nu):
        return i < nu[0]

    def row_blk(i, j, be, bv, nu):
        return (jnp.minimum(i, nu[0] - 1), 0)

    def jj(i, j, nu):
        return jnp.where(live(i, nu), j, nf - 1)

    grid_spec = pltpu.PrefetchScalarGridSpec(
        num_scalar_prefetch=3,
        grid=(nb, nf),
        in_specs=[
            pl.BlockSpec((tm, half_d), row_blk),
            pl.BlockSpec((1, 1, d, tf), lambda i, j, be, bv, nu: (layer, be[i], 0, jj(i, j, nu))),
            pl.BlockSpec((1, 1, d, tf), lambda i, j, be, bv, nu: (layer, be[i], 0, nf + jj(i, j, nu))),
            pl.BlockSpec((1, 1, tf, d), lambda i, j, be, bv, nu: (layer, be[i], jj(i, j, nu), 0)),
            pl.BlockSpec((1, 1, 1, tf), lambda i, j, be, bv, nu: (layer, be[i], 0, jj(i, j, nu))),
            pl.BlockSpec((1, 1, 1, tf), lambda i, j, be, bv, nu: (layer, be[i], 0, nf + jj(i, j, nu))),
            pl.BlockSpec((1, 1, 1, d), lambda i, j, be, bv, nu: (layer, be[i], 0, 0)),
        ],
        out_specs=pl.BlockSpec((tm, d), row_blk),
        scratch_shapes=[pltpu.VMEM((tm, d), BF16)],
    )
    nl = w_gu.shape[0]
    return pl.pallas_call(
        _moe_body,
        grid_spec=grid_spec,
        out_shape=jax.ShapeDtypeStruct((n_slots, d), F32),
        compiler_params=_cparams(("arbitrary", "arbitrary")),
        name="moe",
    )(blk_e, blk_valid, n_used, xb, w_gu, w_gu, w_dn,
      b_gu.reshape(nl, n_exp, 1, two_f), b_gu.reshape(nl, n_exp, 1, two_f), b_dn.reshape(nl, n_exp, 1, d))


def _combine_body(dest_ref, destn_ref, gate_ref, x_ref, mod_ref, fg_ref, yb_ref, oc_ref, ol_ref, buf, sems,
                  *, final, n_ctx_tiles):
    rows = x_ref.shape[0]
    i = pl.program_id(0)
    cur = i % 2

    def row_copy(b, r, k, d):
        return pltpu.make_async_copy(yb_ref.at[pl.ds(d, 1)], buf.at[b, k, pl.ds(r, 1)], sems.at[b])

    def for_rows(b, d_ref, fn):
        def body(g, carry):
            r0 = pl.multiple_of(g * DMA_UNROLL, DMA_UNROLL)
            for u in range(DMA_UNROLL):
                for k in range(TOP_K):
                    fn(row_copy(b, r0 + u, k, d_ref[0, 0, (r0 + u) * TOP_K + k]), k)
            return carry

        lax.fori_loop(0, rows // DMA_UNROLL, body, 0)

    def start(cp, n):
        cp.start(priority=n % 2)

    @pl.when(i == 0)
    def _():
        for_rows(0, dest_ref, start)

    @pl.when(i + 1 < pl.num_programs(0))
    def _():
        for_rows(1 - cur, destn_ref, start)

    for_rows(cur, dest_ref, lambda cp, n: cp.wait())

    ff = gate_ref[:, 0:1] * buf[cur, 0]
    for k in range(1, TOP_K):
        ff = ff + gate_ref[:, k:k + 1] * buf[cur, k]
    x_new = x_ref[...] + mod_ref[0, 5:6, :] * ff
    if final:
        x_new = _rms_lanes(x_new) * fg_ref[...]
    is_ctx = pl.program_id(0) < n_ctx_tiles

    @pl.when(is_ctx)
    def _():
        oc_ref[...] = x_new

    @pl.when(jnp.logical_not(is_ctx))
    def _():
        ol_ref[...] = x_new


def _combine(dest_tiles, gates, x, mod, final_g, yb, *, t_ctx, ds, final):
    t, d = x.shape
    n_ctx_tiles = t_ctx // TILE
    grp = functools.partial(_group_of_block, rows=TILE, t_ctx=t_ctx, ds=ds)
    nt = t // TILE
    return pl.pallas_call(
        functools.partial(_combine_body, final=final, n_ctx_tiles=n_ctx_tiles),
        grid=(nt,),
        in_specs=[
            pl.BlockSpec((1, 1, TILE * TOP_K), lambda i: (i, 0, 0), memory_space=pltpu.SMEM),
            pl.BlockSpec((1, 1, TILE * TOP_K), lambda i: (jnp.minimum(i + 1, nt - 1), 0, 0),
                         memory_space=pltpu.SMEM),
            pl.BlockSpec((TILE, LANES), lambda i: (i, 0)),
            pl.BlockSpec((TILE, d), lambda i: (i, 0)),
            pl.BlockSpec((1, 6, d), lambda i: (grp(i), 0, 0)),
            pl.BlockSpec((1, d), lambda i: (0, 0)),
            pl.BlockSpec(memory_space=pl.ANY),
        ],
        out_specs=_pair_specs(TILE, d, n_ctx_tiles),
        out_shape=[jax.ShapeDtypeStruct((t_ctx, d), F32), jax.ShapeDtypeStruct((t - t_ctx, d), F32)],
        scratch_shapes=[pltpu.VMEM((2, TOP_K, TILE, d), F32), pltpu.SemaphoreType.DMA((2,))],
        compiler_params=_cparams(("arbitrary",)),
        name="combine",
    )(dest_tiles, dest_tiles, gates, x, mod, final_g, yb)


def _slot_layout(idx, rank, counts, n_blocks):
    n_exp = counts.shape[0]
    n_blk = (counts + TM_MOE - 1) // TM_MOE
    padded = n_blk * TM_MOE
    pend = jnp.cumsum(padded)
    pstart = pend - padded
    per = jnp.maximum(-(-counts // jnp.maximum(n_blk, 1)), 1)
    per = (per + SUB_MOE - 1) // SUB_MOE * SUB_MOE
    per_tok = per[idx]
    blk_in = jnp.floor((rank.astype(F32) + 0.5) / per_tok.astype(F32)).astype(I32)
    dest = pstart[idx] + blk_in * TM_MOE + (rank - blk_in * per_tok)
    bstart = jnp.arange(n_blocks, dtype=I32) * TM_MOE
    n_used = (pend[-1] // TM_MOE).astype(I32).reshape(1)
    owner_of = jnp.minimum(bstart, jnp.maximum(pend[-1] - TM_MOE, 0))
    blk_e = jnp.minimum(jnp.searchsorted(pend, owner_of, side='right'), n_exp - 1).astype(I32)
    local = (bstart - pstart[blk_e]) // TM_MOE
    blk_valid = jnp.clip(counts[blk_e] - local * per[blk_e], 0, per[blk_e])
    blk_valid = jnp.where(bstart < pend[-1], blk_valid, 0).astype(I32)
    last_blk = jnp.maximum(counts - 1, 0) // per
    last_cnt = counts - last_blk * per
    last_group = pstart + last_blk * TM_MOE + jnp.maximum((last_cnt + SUB_MOE - 1) // SUB_MOE - 1, 0) * SUB_MOE
    fill = jnp.stack([last_group, (counts > 0).astype(I32)]).astype(I32)
    return dest.astype(I32), fill, blk_e, blk_valid, n_used


def kernel(x_prompt, x_sample, cache_k, cache_v, c, c_ctx, norm1_g, norm2_g, w_ada, b_ada, w_in, w_out,
           lam_q1, lam_k1, lam_q2, lam_k2, diff_ln_g, gmlp_norm_g, gmlp_ws, gmlp_bs, pool_w, pool_scale,
           router_w, router_b, exp_w_gu, exp_b_gu, exp_w_down, exp_b_down, final_g):
    nb_ctx, s_ctx, d = x_prompt.shape
    nb_lat, ds, _ = x_sample.shape
    depth = w_in.shape[0]
    n_exp = router_w.shape[-1]
    p_len = cache_k.shape[2]
    t_ctx, t_lat = nb_ctx * s_ctx, nb_lat * ds
    t = t_ctx + t_lat
    assert t_ctx % TILE == 0 and t_lat % TILE == 0 and (t * TOP_K) % TM_MOE == 0

    x_ctx, x_lat = x_prompt.reshape(t_ctx, d), x_sample.reshape(t_lat, d)
    cond_t = jnp.concatenate([c_ctx[None, :], c], axis=0).T
    mod = _ada(cond_t, w_ada, b_ada)
    mod = mod.reshape(depth, 1 + nb_lat, 6, d)

    rope_tabs = _rope_tables(ds)
    ck = cache_k.reshape(nb_lat, depth, p_len, A_HEADS * HEAD_W)
    cv = cache_v.reshape(nb_lat, depth, p_len, A_WIDTH)
    w_in_b = w_in.astype(BF16)
    w_out_b = w_out.astype(BF16)
    ws_b = gmlp_ws.astype(BF16)
    pw_b = pool_w.astype(BF16)
    rw_pad = jnp.pad(router_w, ((0, 0), (0, 0), (0, LANES - n_exp)))
    rw_hi = rw_pad.astype(BF16)
    rw_parts = jnp.stack([rw_hi, (rw_pad - rw_hi.astype(F32)).astype(BF16)], axis=1)
    rb_pad = jnp.pad(router_b, ((0, 0), (0, LANES - n_exp)))
    n_blocks = t * TOP_K // TM_MOE + n_exp

    caches = None
    for l in range(depth):
        lam_init = 0.8 - 0.6 * math.exp(-0.3 * l)
        lamp = jnp.stack([lam_q1[l], lam_k1[l], lam_q2[l], lam_k2[l]])
        lng = diff_ln_g[l][None, :]
        z = _inproj(x_ctx, x_lat, mod[l], norm1_g[l][None, :], w_in_b[l], ds=ds)
        oa_ctx, *caches = _attn_ctx(z, lamp, lng, caches, layer=l, depth=depth, nb=nb_ctx, s=s_ctx,
                                    lam_init=lam_init)
        oa_lat = _attn_lat(z, ck, cv, rope_tabs, lamp, lng, layer=l, t_ctx=t_ctx, nb=nb_lat, ds=ds,
                           lam_init=lam_init)
        gb = jnp.repeat(gmlp_bs[l].T, B_CH, axis=1)
        x_mid, h_packed, idx, gates, rank, counts = _mixpost(
            oa_ctx, oa_lat, z, x_ctx, x_lat, mod[l], norm2_g[l][None, :], w_out_b[l], gmlp_norm_g[l][None, :],
            ws_b[l], gb, pw_b[l], pool_scale[l][None, :], rw_parts[l], rb_pad[l][None, :], s_ctx=s_ctx, ds=ds,
            n_exp=n_exp)
        dest, fill, blk_e, blk_valid, n_used = _slot_layout(
            idx[:, :TOP_K], rank[:, :TOP_K], counts[0, :n_exp].astype(I32), n_blocks)
        dest_tiles = dest.reshape(t // TILE, 1, TILE * TOP_K)
        xb = _dispatch(fill, dest_tiles, h_packed, n_blocks * TM_MOE)
        yb = _moe(blk_e, blk_valid, n_used, xb, exp_w_gu, exp_b_gu, exp_w_down, exp_b_down, layer=l)
        x_ctx, x_lat = _combine(dest_tiles, gates, x_mid, mod[l], final_g[None, :], yb, t_ctx=t_ctx, ds=ds,
                                final=(l == depth - 1))

    return (x_ctx.reshape(nb_ctx, s_ctx, d), x_lat.reshape(nb_lat, ds, d),
            caches[0].reshape(nb_ctx, depth, s_ctx, A_HEADS, HEAD_W),
            caches[1].reshape(nb_ctx, depth, s_ctx, A_HEADS, V_DIM))
```
